```python
import jax, jax.numpy as jnp
from jax import lax
import numpy as np

D_MODEL = 1024
BATCH = 8
SEQ = 4096
DEPTH = 1
DEC_BATCH = 128
DEC_SEQ = 1
PAST_LEN = 8192
PAGE_SIZE = 128

HEAD_DIM = 64
ATTN_W = D_MODEL // 2
N_HEADS = ATTN_W // HEAD_DIM
CONV_CH = D_MODEL - ATTN_W
CONV_GROUPS = 8
CONV_W = 31
MIX_W = ATTN_W + CONV_CH
Q_BLOCK = 128
PEER_HEADS = 8
PEER_KEYS = 128
PEER_QDIM = 256
PEER_TOPK = 16
N_EXPERTS = PEER_KEYS * PEER_KEYS
PEER_CHUNK = 128
IN_W = 3 * ATTN_W + 2 * CONV_CH + N_HEADS
EPS = 1e-6

kernel_name = 'hymba_fox_conformer_peer_step'


def rmsnorm(x, g):
    xf = x.astype(jnp.float32)
    y = xf * lax.rsqrt(jnp.mean(xf * xf, axis=-1, keepdims=True) + EPS)
    return (y * g.astype(jnp.float32)).astype(x.dtype)


def layernorm(x, g, b):
    xf = x.astype(jnp.float32)
    mu = jnp.mean(xf, axis=-1, keepdims=True)
    var = jnp.mean(jnp.square(xf - mu), axis=-1, keepdims=True)
    y = (xf - mu) * lax.rsqrt(var + EPS)
    return (y * g.astype(jnp.float32) + b.astype(jnp.float32)).astype(x.dtype)


def project(h, w_in, b_forget, q_gain, k_gain):
    proj = h @ w_in
    q, k, v, a, gt, fg = jnp.split(
        proj, [ATTN_W, 2 * ATTN_W, 3 * ATTN_W, 3 * ATTN_W + CONV_CH, 3 * ATTN_W + 2 * CONV_CH], axis=-1)
    lead = h.shape[:-1]
    q = rmsnorm(q.reshape(*lead, N_HEADS, HEAD_DIM), q_gain)
    k = rmsnorm(k.reshape(*lead, N_HEADS, HEAD_DIM), k_gain)
    v = v.reshape(*lead, N_HEADS, HEAD_DIM)
    u = a * jax.nn.sigmoid(gt)
    logf = jax.nn.log_sigmoid((fg + b_forget).astype(jnp.float32))
    return q, k, v, u, logf


def fox_prompt(q, k, v, logf):
    B, S = q.shape[0], q.shape[1]
    nb = S // Q_BLOCK
    scale = HEAD_DIM ** -0.5
    c = jnp.cumsum(logf, axis=1).transpose(0, 2, 1)
    key_pos = jnp.arange(S)
    qb = q.reshape(B, nb, Q_BLOCK, N_HEADS, HEAD_DIM).transpose(1, 0, 2, 3, 4)
    cb = c.reshape(B, N_HEADS, nb, Q_BLOCK).transpose(2, 0, 1, 3)

    def block(args):
        qi, ci, bi = args
        q_pos = bi * Q_BLOCK + jnp.arange(Q_BLOCK)
        s = jnp.einsum('bqhd,bkhd->bhqk', qi, k).astype(jnp.float32) * scale
        s = s + ci[..., :, None] - c[..., None, :]
        s = jnp.where(key_pos[None, :] <= q_pos[:, None], s, -jnp.inf)
        p = jax.nn.softmax(s, axis=-1)
        return jnp.einsum('bhqk,bkhd->bqhd', p.astype(v.dtype), v)

    o = lax.map(block, (qb, cb, jnp.arange(nb)))
    return o.transpose(1, 0, 2, 3, 4).reshape(B, S, N_HEADS, HEAD_DIM)


def fox_sample(q, k, v, logf, cache_k, cache_v, cache_logf, page_table):
    T = q.shape[1]
    page = cache_k.shape[1]
    past = page_table.shape[1] * page
    scale = HEAD_DIM ** -0.5
    key_pos = jnp.arange(past + T)
    q_pos = past + jnp.arange(T)
    mask = key_pos[None, :] <= q_pos[:, None]

    def one(args):
        qi, ki, vi, lfi, pt = args
        kk = jnp.concatenate([cache_k[pt].reshape(past, N_HEADS, HEAD_DIM).astype(ki.dtype), ki], axis=0)
        vv = jnp.concatenate([cache_v[pt].reshape(past, N_HEADS, HEAD_DIM).astype(vi.dtype), vi], axis=0)
        lf = jnp.concatenate([cache_logf[pt].reshape(past, N_HEADS).astype(jnp.float32), lfi], axis=0)
        c = jnp.cumsum(lf, axis=0).T
        s = jnp.einsum('qhd,khd->hqk', qi, kk).astype(jnp.float32) * scale
        s = s + c[:, past:, None] - c[:, None, :]
        s = jnp.where(mask[None], s, -jnp.inf)
        p = jax.nn.softmax(s, axis=-1)
        return jnp.einsum('hqk,khd->qhd', p.astype(vv.dtype), vv)

    return lax.map(one, (q, k, v, logf, page_table))


def conv_module(u, past, conv_w, conv_b, ln_g, ln_b):
    xp = jnp.concatenate([past.astype(u.dtype), u], axis=1)
    y = lax.conv_general_dilated(
        xp, conv_w[:, None, :].astype(u.dtype), window_strides=(1,), padding='VALID',
        dimension_numbers=('NWC', 'WIO', 'NWC'), feature_group_count=CONV_CH) + conv_b
    y = jax.nn.silu(layernorm(y, ln_g, ln_b))
    return y, xp[:, -(CONV_W - 1):]


def peer(h, wq, subkeys, u_tab, v_tab):
    lead = h.shape[:-1]
    hf = h.reshape(-1, D_MODEL)
    n = hf.shape[0]
    hf = jnp.pad(hf, ((0, (-n) % PEER_CHUNK), (0, 0)))
    chunks = hf.reshape(-1, PEER_CHUNK, D_MODEL)

    def chunk(hc):
        q = (hc @ wq).reshape(PEER_CHUNK, PEER_HEADS, 2, PEER_QDIM // 2)
        sc = jnp.einsum('chpd,hpnd->chpn', q, subkeys).astype(jnp.float32)
        s1, i1 = lax.top_k(sc[:, :, 0], PEER_TOPK)
        s2, i2 = lax.top_k(sc[:, :, 1], PEER_TOPK)
        cand_s = (s1[..., :, None] + s2[..., None, :]).reshape(PEER_CHUNK, PEER_HEADS, PEER_TOPK * PEER_TOPK)
        cand_id = (i1[..., :, None] * PEER_KEYS + i2[..., None, :]).reshape(PEER_CHUNK, PEER_HEADS, PEER_TOPK * PEER_TOPK)
        best_s, best_j = lax.top_k(cand_s, PEER_TOPK)
        eid = jnp.take_along_axis(cand_id, best_j, axis=-1)
        g = jax.nn.softmax(best_s, axis=-1)
        act = jax.nn.gelu(jnp.einsum('cd,chkd->chk', hc, u_tab[eid]).astype(jnp.float32), approximate=False)
        return jnp.einsum('chk,chkd->cd', (g * act).astype(hc.dtype), v_tab[eid])

    out = lax.map(chunk, chunks).reshape(-1, D_MODEL)[:n]
    return out.reshape(*lead, D_MODEL)


def layer_step(x_p, x_s, ck, cv, clf, sconv, page_table, norm1_g, w_in, b_forget, q_gain, k_gain,
               conv_w, conv_b, conv_ln_g, conv_ln_b, w_out, norm2_g, peer_wq, peer_subkeys, peer_u, peer_v):
    qp, kp, vp, up, lfp = project(rmsnorm(x_p, norm1_g), w_in, b_forget, q_gain, k_gain)
    ap = fox_prompt(qp, kp, vp, lfp)
    cp, conv_state_p = conv_module(up, jnp.zeros((x_p.shape[0], CONV_W - 1, CONV_CH), up.dtype),
                                   conv_w, conv_b, conv_ln_g, conv_ln_b)
    qs, ks, vs, us, lfs = project(rmsnorm(x_s, norm1_g), w_in, b_forget, q_gain, k_gain)
    a_s = fox_sample(qs, ks, vs, lfs, ck, cv, clf, page_table)
    cs, conv_state_s = conv_module(us, sconv, conv_w, conv_b, conv_ln_g, conv_ln_b)

    def finish(x, a, c):
        mix = jnp.concatenate([a.reshape(*a.shape[:2], ATTN_W), c], axis=-1)
        y = x + mix @ w_out
        return y + peer(rmsnorm(y, norm2_g), peer_wq, peer_subkeys, peer_u, peer_v)

    y_p = finish(x_p, ap, cp)
    y_s = finish(x_s, a_s, cs)
    return y_p, y_s, (kp, vp, lfp, conv_state_p, ks, vs, lfs, conv_state_s)


def setup_inputs(seed: int = 0) -> dict:
    key = jax.random.key(seed)
    ks = jax.random.split(key, 24)
    f32 = jnp.float32
    n_pages = PAST_LEN // PAGE_SIZE
    n_used = DEC_BATCH * n_pages
    n_pool = n_used + n_used // 4
    nrm = lambda k, shape, s: jax.random.normal(k, shape, f32) * s
    page_table = jax.random.permutation(ks[0], n_pool)[:n_used].reshape(DEC_BATCH, n_pages).astype(jnp.int32)
    return {
        'x_prompt': nrm(ks[1], (BATCH, SEQ, D_MODEL), 1.0),
        'x_sample': nrm(ks[2], (DEC_BATCH, DEC_SEQ, D_MODEL), 1.0),
        'cache_k': nrm(ks[3], (DEPTH, n_pool, PAGE_SIZE, N_HEADS, HEAD_DIM), 1.0),
        'cache_v': nrm(ks[4], (DEPTH, n_pool, PAGE_SIZE, N_HEADS, HEAD_DIM), 1.0),
        'cache_logf': jax.nn.log_sigmoid(3.0 + nrm(ks[5], (DEPTH, n_pool, PAGE_SIZE, N_HEADS), 1.0)),
        'state_conv': nrm(ks[6], (DEPTH, DEC_BATCH, CONV_W - 1, CONV_CH), 0.5),
        'page_table': page_table,
        'norm1_g': 1.0 + nrm(ks[7], (DEPTH, D_MODEL), 0.02),
        'w_in': nrm(ks[8], (DEPTH, D_MODEL, IN_W), D_MODEL ** -0.5),
        'b_forget': 3.0 + nrm(ks[9], (DEPTH, N_HEADS), 0.1),
        'q_gain': 1.0 + nrm(ks[10], (DEPTH, HEAD_DIM), 0.02),
        'k_gain': 1.0 + nrm(ks[11], (DEPTH, HEAD_DIM), 0.02),
        'conv_w': nrm(ks[12], (DEPTH, CONV_W, CONV_CH), CONV_W ** -0.5),
        'conv_b': nrm(ks[13], (DEPTH, CONV_CH), 0.02),
        'conv_ln_g': 1.0 + nrm(ks[14], (DEPTH, CONV_CH), 0.02),
        'conv_ln_b': nrm(ks[15], (DEPTH, CONV_CH), 0.02),
        'w_out': nrm(ks[16], (DEPTH, MIX_W, D_MODEL), MIX_W ** -0.5),
        'norm2_g': 1.0 + nrm(ks[17], (DEPTH, D_MODEL), 0.02),
        'peer_wq': nrm(ks[18], (DEPTH, D_MODEL, PEER_HEADS * PEER_QDIM), D_MODEL ** -0.5),
        'peer_subkeys': nrm(ks[19], (DEPTH, PEER_HEADS, 2, PEER_KEYS, PEER_QDIM // 2), (PEER_QDIM // 2) ** -0.5),
        'peer_u': nrm(ks[20], (DEPTH, N_EXPERTS, D_MODEL), D_MODEL ** -0.5),
        'peer_v': nrm(ks[21], (DEPTH, N_EXPERTS, D_MODEL), PEER_HEADS ** -0.5),
    }


def reference(x_prompt, x_sample, cache_k, cache_v, cache_logf, state_conv, page_table,
              norm1_g, w_in, b_forget, q_gain, k_gain, conv_w, conv_b, conv_ln_g, conv_ln_b,
              w_out, norm2_g, peer_wq, peer_subkeys, peer_u, peer_v):
    xp, xs = x_prompt, x_sample
    states = []
    for l in range(DEPTH):
        xp, xs, st = layer_step(
            xp, xs, cache_k[l], cache_v[l], cache_logf[l], state_conv[l], page_table,
            norm1_g[l], w_in[l], b_forget[l], q_gain[l], k_gain[l], conv_w[l], conv_b[l],
            conv_ln_g[l], conv_ln_b[l], w_out[l], norm2_g[l], peer_wq[l], peer_subkeys[l],
            peer_u[l], peer_v[l])
        states.append(st)
    k_prompt = jnp.stack([s[0] for s in states])
    v_prompt = jnp.stack([s[1] for s in states])
    logf_prompt = jnp.stack([s[2] for s in states])
    conv_prompt = jnp.stack([s[3] for s in states])
    k_sample = jnp.stack([s[4] for s in states])
    v_sample = jnp.stack([s[5] for s in states])
    logf_sample = jnp.stack([s[6] for s in states])
    conv_sample = jnp.stack([s[7] for s in states])
    return (xp, xs, k_prompt, v_prompt, logf_prompt, conv_prompt, k_sample, v_sample, logf_sample, conv_sample)
```

```python
import functools
import math

import jax
import jax.numpy as jnp
from jax import lax
from jax.experimental import pallas as pl
from jax.experimental.pallas import tpu as pltpu

F32 = jnp.float32
BF16 = jnp.bfloat16

D_MODEL = 1024
HEAD_DIM = 64
N_HEADS = 8
ATTN_W = N_HEADS * HEAD_DIM
CONV_CH = D_MODEL - ATTN_W
CONV_W = 31
PEER_HEADS = 8
PEER_KEYS = 128
PEER_TOPK = 16
N_EXPERTS = PEER_KEYS * PEER_KEYS
EPS = 1e-6
LANES = 128
SUBLANES = 8
VMEM_LIMIT = 56 * 1024 * 1024
NEG_INF = float("-inf")


def _cparams(sem):
    return pltpu.CompilerParams(dimension_semantics=sem, vmem_limit_bytes=VMEM_LIMIT)


def _split3(x):
    p1 = x.astype(BF16)
    r1 = x - p1.astype(F32)
    p2 = r1.astype(BF16)
    p3 = (r1 - p2.astype(F32)).astype(BF16)
    return p1, p2, p3


def _full_spec(shape):
    return pl.BlockSpec(shape, lambda *_: (0,) * len(shape))


def _group_rms(t, bd, gain):
    sq = t * t
    hi = sq.astype(BF16)
    lo = (sq - hi.astype(F32)).astype(BF16)
    ss = jnp.dot(hi, bd, preferred_element_type=F32) + jnp.dot(lo, bd, preferred_element_type=F32)
    return t * lax.rsqrt(ss * (1.0 / HEAD_DIM) + EPS) * gain


def _proj_kernel(x_ref, g1_ref, wqkv_ref, wag_ref, wfg_ref, bf_ref, qg_ref, kg_ref, bd_ref, tri_ref,
                 k_ref, v_ref, u_ref, lf_ref, qb_ref, kb_ref, vb_ref, ccol_ref, crow_ref,
                 carry_ref, *, tiles_per_seq, with_cumsum):
    t = x_ref.shape[0]
    x = x_ref[...]
    ms = jnp.mean(x * x, axis=-1, keepdims=True)
    h = (x * lax.rsqrt(ms + EPS) * g1_ref[...]).astype(BF16)

    qkv = jnp.dot(h, wqkv_ref[...], preferred_element_type=F32)
    bd = bd_ref[...]
    q = _group_rms(qkv[:, :ATTN_W], bd, qg_ref[...])
    k = _group_rms(qkv[:, ATTN_W:2 * ATTN_W], bd, kg_ref[...])
    v = qkv[:, 2 * ATTN_W:]
    k_ref[...] = k
    v_ref[...] = v
    qb_ref[...] = (q * (HEAD_DIM ** -0.5)).astype(BF16)
    kb_ref[...] = k.astype(BF16)
    vb_ref[...] = v.astype(BF16)

    ag = jnp.dot(h, wag_ref[...], preferred_element_type=F32)
    u_ref[...] = ag[:, :CONV_CH] * jax.nn.sigmoid(ag[:, CONV_CH:])

    z = jnp.dot(h, wfg_ref[...], preferred_element_type=F32) + bf_ref[...]
    logf = jnp.minimum(z, 0.0) - jnp.log1p(jnp.exp(-jnp.abs(z)))
    lf_ref[...] = logf[:, :N_HEADS]

    if with_cumsum:
        i = pl.program_id(0)
        lane = lax.broadcasted_iota(jnp.int32, (1, LANES), 1)
        carry = jnp.where(i % tiles_per_seq == 0, 0.0, carry_ref[...])
        tri = tri_ref[...]
        for sb in range(t // LANES):
            blk = jnp.where(lane < N_HEADS, logf[sb * LANES:(sb + 1) * LANES], 0.0)
            p1, p2, p3 = _split3(blk)
            cs = (jnp.dot(tri, p1, preferred_element_type=F32)
                  + jnp.dot(tri, p2, preferred_element_type=F32)
                  + jnp.dot(tri, p3, preferred_element_type=F32)) + carry
            carry = cs[LANES - 1:LANES, :]
            ccol_ref[sb * LANES:(sb + 1) * LANES, :] = cs[:, :N_HEADS]
            crow_ref[:, sb * LANES:(sb + 1) * LANES] = cs.T[:N_HEADS, :]
        carry_ref[...] = carry
    else:
        ccol_ref[...] = jnp.zeros_like(ccol_ref)
        crow_ref[...] = jnp.zeros_like(crow_ref)


def _proj(x2d, wts, *, tile, seq_len, with_cumsum):
    n = x2d.shape[0]
    nt = n // tile
    row = lambda w: pl.BlockSpec((tile, w), lambda i: (i, 0))
    kern = functools.partial(_proj_kernel, tiles_per_seq=max(seq_len // tile, 1), with_cumsum=with_cumsum)
    out_shape = (
        jax.ShapeDtypeStruct((n, ATTN_W), F32),
        jax.ShapeDtypeStruct((n, ATTN_W), F32),
        jax.ShapeDtypeStruct((n, CONV_CH), F32),
        jax.ShapeDtypeStruct((n, N_HEADS), F32),
        jax.ShapeDtypeStruct((n, ATTN_W), BF16),
        jax.ShapeDtypeStruct((n, ATTN_W), BF16),
        jax.ShapeDtypeStruct((n, ATTN_W), BF16),
        jax.ShapeDtypeStruct((n, N_HEADS), F32),
        jax.ShapeDtypeStruct((N_HEADS, n), F32),
    )
    out_specs = (row(ATTN_W), row(ATTN_W), row(CONV_CH), row(N_HEADS), row(ATTN_W), row(ATTN_W), row(ATTN_W),
                 row(N_HEADS), pl.BlockSpec((N_HEADS, tile), lambda i: (0, i)))
    in_specs = [row(D_MODEL)] + [_full_spec(w.shape) for w in wts]
    return pl.pallas_call(
        kern, grid=(nt,), in_specs=in_specs, out_specs=out_specs, out_shape=out_shape,
        scratch_shapes=[pltpu.VMEM((1, LANES), F32)],
        compiler_params=_cparams(("arbitrary",)), name="proj",
    )(x2d, *wts)


def _attn_kernel(q_ref, k_ref, v_ref, ccol_ref, crow_ref, o_ref, m_ref, l_ref, acc_ref, *, tq):
    p = pl.program_id(1)
    qi = pl.program_id(2)
    lane = lax.broadcasted_iota(jnp.int32, (1, LANES), 1)
    lane8 = lax.broadcasted_iota(jnp.int32, (1, N_HEADS), 1)
    q = q_ref[...].astype(F32)
    ccol = ccol_ref[...]
    row_id = lax.broadcasted_iota(jnp.int32, (tq, tq), 0)
    col_id = lax.broadcasted_iota(jnp.int32, (tq, tq), 1)
    outs = []
    for hh in range(2):
        h = 2 * p + hh
        in_head = (lane >= hh * HEAD_DIM) & (lane < (hh + 1) * HEAD_DIM)
        qh = jnp.where(in_head, q, 0.0).astype(BF16)
        cq = jnp.sum(jnp.where(lane8 == h, ccol, 0.0), axis=-1, keepdims=True)
        m_ref[...] = jnp.full_like(m_ref, NEG_INF)
        l_ref[...] = jnp.zeros_like(l_ref)
        acc_ref[...] = jnp.zeros_like(acc_ref)

        def step(j, masked):
            start = pl.multiple_of(j * tq, tq)
            kt = k_ref[pl.ds(start, tq), :]
            s = lax.dot_general(qh, kt, (((1,), (1,)), ((), ())), preferred_element_type=F32)
            s = s + (cq - crow_ref[pl.ds(h, 1), pl.ds(start, tq)])
            if masked:
                s = jnp.where(col_id <= row_id, s, NEG_INF)
            m_old = m_ref[...]
            m_new = jnp.maximum(m_old, jnp.max(s, axis=-1, keepdims=True))
            alpha = jnp.exp(m_old - m_new)
            pr = jnp.exp(s - m_new)
            l_ref[...] = alpha * l_ref[...] + jnp.sum(pr, axis=-1, keepdims=True)
            acc_ref[...] = alpha * acc_ref[...] + jnp.dot(
                pr.astype(BF16), v_ref[pl.ds(start, tq), :], preferred_element_type=F32)
            m_ref[...] = m_new

        def body(j, c):
            step(j, False)
            return c

        lax.fori_loop(0, qi, body, 0)
        step(qi, True)
        outs.append(acc_ref[...] / l_ref[...])
    o_ref[...] = jnp.where(lane < HEAD_DIM, outs[0], outs[1]).astype(o_ref.dtype)


def _attn(qb, kb, vb, ccol, crow, *, batch, seq_len, tq):
    n = qb.shape[0]
    nq = seq_len // tq
    kern = functools.partial(_attn_kernel, tq=tq)
    return pl.pallas_call(
        kern, grid=(batch, N_HEADS // 2, nq),
        in_specs=[
            pl.BlockSpec((tq, LANES), lambda b, p, i: (b * nq + i, p)),
            pl.BlockSpec((seq_len, LANES), lambda b, p, i: (b, p)),
            pl.BlockSpec((seq_len, LANES), lambda b, p, i: (b, p)),
            pl.BlockSpec((tq, N_HEADS), lambda b, p, i: (b * nq + i, 0)),
            pl.BlockSpec((N_HEADS, seq_len), lambda b, p, i: (0, b)),
        ],
        out_specs=pl.BlockSpec((tq, LANES), lambda b, p, i: (b * nq + i, p)),
        out_shape=jax.ShapeDtypeStruct((n, ATTN_W), BF16),
        scratch_shapes=[pltpu.VMEM((tq, 1), F32), pltpu.VMEM((tq, 1), F32), pltpu.VMEM((tq, LANES), F32)],
        compiler_params=_cparams(("arbitrary", "arbitrary", "arbitrary")), name="attn",
    )(qb, kb, vb, ccol, crow)


def _decode_kernel(pt_ref, q_ref, kn_ref, vn_ref, lfn_ref, su_ref, *rest, pages_per_step):
    del pt_ref
    pp = pages_per_step
    k_refs = rest[:pp]
    v_refs = rest[pp:2 * pp]
    lf_refs = rest[2 * pp:3 * pp]
    o_ref, m_ref, l_ref, acc_ref, carry_ref = rest[3 * pp:]
    j = pl.program_id(1)
    nj = pl.num_programs(1)

    sub = lax.broadcasted_iota(jnp.int32, (N_HEADS, ATTN_W), 0)
    lane = lax.broadcasted_iota(jnp.int32, (N_HEADS, ATTN_W), 1)
    headmask = (lane // HEAD_DIM) == sub
    qbd = jnp.where(headmask, jnp.broadcast_to(q_ref[...].astype(F32), (N_HEADS, ATTN_W)), 0.0)
    qbd = qbd.astype(BF16)

    @pl.when(j == 0)
    def _():
        s_self = jnp.sum(qbd.astype(F32) * kn_ref[...].astype(F32), axis=-1, keepdims=True)
        m_ref[...] = s_self
        l_ref[...] = jnp.ones_like(l_ref)
        acc_ref[...] = jnp.where(headmask, jnp.broadcast_to(vn_ref[...].astype(F32), (N_HEADS, ATTN_W)), 0.0)
        carry_ref[...] = lfn_ref[...]

    lfs = [r[...] for r in lf_refs]
    parts = []
    for lf in lfs:
        parts.extend(_split3(lf))
    suf = jnp.dot(jnp.concatenate(parts, axis=0), su_ref[...], preferred_element_type=F32)
    carry = carry_ref[...]
    ss = []
    for i in range(pp):
        base = 3 * N_HEADS * i
        within = suf[base:base + 8] + suf[base + 8:base + 16] + suf[base + 16:base + 24]
        kt = k_refs[i][...].astype(BF16)
        s = lax.dot_general(qbd, kt, (((1,), (1,)), ((), ())), preferred_element_type=F32)
        ss.append(s + within + carry)
        carry = carry + jnp.sum(lfs[i], axis=-1, keepdims=True)
    carry_ref[...] = carry

    m_old = m_ref[...]
    m_new = m_old
    for s in ss:
        m_new = jnp.maximum(m_new, jnp.max(s, axis=-1, keepdims=True))
    alpha = jnp.exp(m_old - m_new)
    l_new = alpha * l_ref[...]
    acc = alpha * acc_ref[...]
    for i in range(pp):
        pr = jnp.exp(ss[i] - m_new)
        l_new = l_new + jnp.sum(pr, axis=-1, keepdims=True)
        acc = acc + jnp.dot(pr.astype(BF16), v_refs[i][...].astype(BF16), preferred_element_type=F32)
    m_ref[...] = m_new
    l_ref[...] = l_new
    acc_ref[...] = acc

    @pl.when(j == nj - 1)
    def _():
        o = jnp.where(headmask, acc / l_new, 0.0)
        o_ref[...] = jnp.sum(o, axis=0, keepdims=True).astype(o_ref.dtype)


def _decode(page_table, qb, kb, vb, lf_new, cache_k, cache_v, cache_lf_t, *, pages_per_step):
    db, n_pages = page_table.shape
    pp = pages_per_step
    nj = n_pages // pp
    page = cache_k.shape[1]
    su = (lax.broadcasted_iota(jnp.int32, (page, page), 0) >
          lax.broadcasted_iota(jnp.int32, (page, page), 1)).astype(BF16)

    def page_idx(i):
        return lambda b, j, pt: (pt[b, n_pages - 1 - (j * pp + i)], 0, 0)

    per_seq = lambda w: pl.BlockSpec((None, 1, w), lambda b, j, pt: (b, 0, 0))
    in_specs = [per_seq(ATTN_W), per_seq(ATTN_W), per_seq(ATTN_W),
                pl.BlockSpec((None, N_HEADS, 1), lambda b, j, pt: (b, 0, 0)),
                pl.BlockSpec((page, page), lambda b, j, pt: (0, 0))]
    in_specs += [pl.BlockSpec((None, page, ATTN_W), page_idx(i)) for i in range(pp)]
    in_specs += [pl.BlockSpec((None, page, ATTN_W), page_idx(i)) for i in range(pp)]
    in_specs += [pl.BlockSpec((None, N_HEADS, page), page_idx(i)) for i in range(pp)]
    grid_spec = pltpu.PrefetchScalarGridSpec(
        num_scalar_prefetch=1, grid=(db, nj), in_specs=in_specs,
        out_specs=pl.BlockSpec((None, 1, ATTN_W), lambda b, j, pt: (b, 0, 0)),
        scratch_shapes=[pltpu.VMEM((N_HEADS, 1), F32), pltpu.VMEM((N_HEADS, 1), F32),
                        pltpu.VMEM((N_HEADS, ATTN_W), F32), pltpu.VMEM((N_HEADS, 1), F32)])
    kern = functools.partial(_decode_kernel, pages_per_step=pp)
    return pl.pallas_call(
        kern, grid_spec=grid_spec, out_shape=jax.ShapeDtypeStruct((db, 1, ATTN_W), BF16),
        compiler_params=_cparams(("arbitrary", "arbitrary")), name="decode",
    )(page_table, qb, kb, vb, lf_new, su, *([cache_k] * pp), *([cache_v] * pp), *([cache_lf_t] * pp))


CONV_HALO = 32
CONV_ROWS = 32


def _ln_silu(y, g, b):
    mu = jnp.mean(y, axis=-1, keepdims=True)
    d = y - mu
    var = jnp.mean(d * d, axis=-1, keepdims=True)
    z = d * lax.rsqrt(var + EPS) * g + b
    return z * jax.nn.sigmoid(z)


def _conv_kernel(cur_ref, prev_ref, w_ref, b_ref, g_ref, lb_ref, o_ref, xp_ref, *, tiles_per_seq):
    t = cur_ref.shape[0]
    i = pl.program_id(0)
    xp_ref[0:CONV_HALO, :] = jnp.where(i % tiles_per_seq == 0, 0.0, prev_ref[...])
    xp_ref[CONV_HALO:, :] = cur_ref[...]
    off = CONV_HALO - (CONV_W - 1)
    for c in range(t // CONV_ROWS):
        acc = jnp.broadcast_to(b_ref[...], (CONV_ROWS, CONV_CH))
        for w in range(CONV_W):
            acc = acc + xp_ref[pl.ds(c * CONV_ROWS + off + w, CONV_ROWS), :] * w_ref[w:w + 1, :]
        o_ref[c * CONV_ROWS:(c + 1) * CONV_ROWS, :] = _ln_silu(acc, g_ref[...], lb_ref[...]).astype(o_ref.dtype)


def _conv_prompt(u2d, conv_w, conv_b, ln_g, ln_b, *, tile, seq_len):
    n = u2d.shape[0]
    ratio = tile // CONV_HALO
    kern = functools.partial(_conv_kernel, tiles_per_seq=seq_len // tile)
    wts = (conv_w, conv_b, ln_g, ln_b)
    return pl.pallas_call(
        kern, grid=(n // tile,),
        in_specs=[pl.BlockSpec((tile, CONV_CH), lambda i: (i, 0)),
                  pl.BlockSpec((CONV_HALO, CONV_CH), lambda i: (jnp.maximum(i * ratio - 1, 0), 0))]
                 + [_full_spec(w.shape) for w in wts],
        out_specs=pl.BlockSpec((tile, CONV_CH), lambda i: (i, 0)),
        out_shape=jax.ShapeDtypeStruct((n, CONV_CH), BF16),
        scratch_shapes=[pltpu.VMEM((tile + CONV_HALO, CONV_CH), F32)],
        compiler_params=_cparams(("arbitrary",)), name="conv_prompt",
    )(u2d, u2d, *wts)


def _conv_step_kernel(xp_ref, w_ref, b_ref, g_ref, lb_ref, o_ref):
    acc = jnp.broadcast_to(b_ref[...], o_ref.shape)
    for w in range(CONV_W):
        acc = acc + xp_ref[w] * w_ref[w:w + 1, :]
    o_ref[...] = _ln_silu(acc, g_ref[...], lb_ref[...]).astype(o_ref.dtype)


def _conv_step(xp_t, conv_w, conv_b, ln_g, ln_b):
    db = xp_t.shape[1]
    args = (xp_t, conv_w, conv_b, ln_g, ln_b)
    return pl.pallas_call(
        _conv_step_kernel, grid=(1,),
        in_specs=[_full_spec(a.shape) for a in args],
        out_specs=_full_spec((db, CONV_CH)),
        out_shape=jax.ShapeDtypeStruct((db, CONV_CH), BF16),
        compiler_params=_cparams(("arbitrary",)), name="conv_step",
    )(*args)


N_CAND = PEER_TOPK + (PEER_TOPK - 1) * SUBLANES


def _top16(s, iota_f, val_ref, idx_ref):
    for r in range(PEER_TOPK):
        m = jnp.max(s, axis=0, keepdims=True)
        idx = jnp.min(jnp.where(s == m, iota_f, float(1 << 20)), axis=0, keepdims=True)
        val_ref[r:r + 1, :] = m
        idx_ref[r:r + 1, :] = idx
        s = jnp.where(iota_f == idx, NEG_INF, s)


def _route_kernel(x_ref, a_ref, c_ref, woa_ref, woc_ref, g2_ref, wq_ref, sk_ref,
                  y_ref, h_ref, i1_ref, i2_ref, g_ref,
                  sc_ref, tv_ref, ti_ref, cand_ref, bs_ref, o1_ref, o2_ref, og_ref):
    t = x_ref.shape[0]
    y = (x_ref[...] + jnp.dot(a_ref[...], woa_ref[...], preferred_element_type=F32)
         + jnp.dot(c_ref[...], woc_ref[...], preferred_element_type=F32))
    y_ref[...] = y
    ms = jnp.mean(y * y, axis=-1, keepdims=True)
    h = (y * lax.rsqrt(ms + EPS) * g2_ref[...]).astype(BF16)
    h_ref[...] = h
    q = jnp.dot(h, wq_ref[...], preferred_element_type=F32).astype(BF16)
    for hp in range(2 * PEER_HEADS):
        sc_ref[hp] = lax.dot_general(sk_ref[hp], q[:, hp * PEER_KEYS:(hp + 1) * PEER_KEYS],
                                     (((1,), (1,)), ((), ())), preferred_element_type=F32)

    key_iota = lax.broadcasted_iota(jnp.int32, (PEER_KEYS, t), 0).astype(F32)

    def stage1(hp, c):
        _top16(sc_ref[hp], key_iota, tv_ref.at[hp], ti_ref.at[hp])
        return c

    lax.fori_loop(0, 2 * PEER_HEADS, stage1, 0)

    rowc = lax.broadcasted_iota(jnp.int32, (N_CAND, t), 0)
    r1c = jnp.where(rowc < PEER_TOPK, 0, (rowc - PEER_TOPK) // SUBLANES + 1)
    r2c = jnp.where(rowc < PEER_TOPK, rowc, (rowc - PEER_TOPK) % SUBLANES)
    cand_id = (r1c * PEER_TOPK + r2c).astype(F32)
    rank_iota = lax.broadcasted_iota(jnp.int32, (PEER_TOPK, t), 0).astype(F32)

    def stage2(hd, c):
        v1 = tv_ref[2 * hd]
        v2 = tv_ref[2 * hd + 1]
        i1 = ti_ref[2 * hd]
        i2 = ti_ref[2 * hd + 1]
        cand_ref[0:PEER_TOPK, :] = v1[0:1, :] + v2
        for r1 in range(1, PEER_TOPK):
            lo = PEER_TOPK + (r1 - 1) * SUBLANES
            cand_ref[lo:lo + SUBLANES, :] = v1[r1:r1 + 1, :] + v2[0:SUBLANES, :]
        s = cand_ref[...]
        base = pl.multiple_of(hd * PEER_TOPK, PEER_TOPK)
        for r in range(PEER_TOPK):
            m = jnp.max(s, axis=0, keepdims=True)
            cid = jnp.min(jnp.where(s == m, cand_id, float(1 << 20)), axis=0, keepdims=True)
            s = jnp.where(cand_id == cid, NEG_INF, s)
            r1 = jnp.floor(cid * (1.0 / PEER_TOPK))
            r2 = cid - r1 * PEER_TOPK
            bs_ref[r:r + 1, :] = m
            o1_ref[pl.ds(base + r, 1), :] = jnp.max(jnp.where(rank_iota == r1, i1, -1.0), axis=0, keepdims=True)
            o2_ref[pl.ds(base + r, 1), :] = jnp.max(jnp.where(rank_iota == r2, i2, -1.0), axis=0, keepdims=True)
        bs = bs_ref[...]
        e = jnp.exp(bs - bs[0:1, :])
        og_ref[pl.ds(base, PEER_TOPK), :] = e / jnp.sum(e, axis=0, keepdims=True)
        return c

    lax.fori_loop(0, PEER_HEADS, stage2, 0)
    i1_ref[...] = o1_ref[...].T
    i2_ref[...] = o2_ref[...].T
    g_ref[...] = og_ref[...].T


def _route(x2d, a2d, c2d, wts, *, tile):
    n = x2d.shape[0]
    row = lambda w: pl.BlockSpec((tile, w), lambda i: (i, 0))
    hk = PEER_HEADS * PEER_TOPK
    out_shape = (jax.ShapeDtypeStruct((n, D_MODEL), F32),
                 jax.ShapeDtypeStruct((n, D_MODEL), BF16),
                 jax.ShapeDtypeStruct((n, hk), F32),
                 jax.ShapeDtypeStruct((n, hk), F32),
                 jax.ShapeDtypeStruct((n, hk), F32))
    return pl.pallas_call(
        _route_kernel, grid=(n // tile,),
        in_specs=[row(D_MODEL), row(ATTN_W), row(CONV_CH)] + [_full_spec(w.shape) for w in wts],
        out_specs=(row(D_MODEL), row(D_MODEL), row(hk), row(hk), row(hk)),
        out_shape=out_shape,
        scratch_shapes=[pltpu.VMEM((2 * PEER_HEADS, PEER_KEYS, tile), F32),
                        pltpu.VMEM((2 * PEER_HEADS, PEER_TOPK, tile), F32),
                        pltpu.VMEM((2 * PEER_HEADS, PEER_TOPK, tile), F32),
                        pltpu.VMEM((N_CAND, tile), F32),
                        pltpu.VMEM((PEER_TOPK, tile), F32),
                        pltpu.VMEM((hk, tile), F32), pltpu.VMEM((hk, tile), F32), pltpu.VMEM((hk, tile), F32)],
        compiler_params=_cparams(("arbitrary",)), name="route",
    )(x2d, a2d, c2d, *wts)


def _gates_kernel(i1_ref, i2_ref, g_ref, o_ref):
    t = i1_ref.shape[0]
    hk = i1_ref.shape[1]
    sub = lax.broadcasted_iota(jnp.int32, (PEER_KEYS, hk), 0).astype(F32)

    def body(tok, c):
        i1 = i1_ref[pl.ds(tok, 1), :]
        i2 = i2_ref[pl.ds(tok, 1), :]
        g = g_ref[pl.ds(tok, 1), :]
        p1 = jnp.where(sub == i1, g, 0.0).astype(BF16)
        p2 = jnp.where(sub == i2, 1.0, 0.0).astype(BF16)
        o_ref[tok] = lax.dot_general(p1, p2, (((1,), (1,)), ((), ())),
                                     preferred_element_type=F32).astype(o_ref.dtype)
        return c

    lax.fori_loop(0, t, body, 0)


def _gates(i1, i2, g, *, tile):
    n, hk = i1.shape
    row = pl.BlockSpec((tile, hk), lambda i: (i, 0))
    return pl.pallas_call(
        _gates_kernel, grid=(n // tile,), in_specs=[row, row, row],
        out_specs=pl.BlockSpec((tile, PEER_KEYS, PEER_KEYS), lambda i: (i, 0, 0)),
        out_shape=jax.ShapeDtypeStruct((n, PEER_KEYS, PEER_KEYS), BF16),
        compiler_params=_cparams(("arbitrary",)), name="gates",
    )(i1, i2, g)


def _experts_kernel(h_ref, g_ref, u_ref, v_ref, y_ref, o_ref, acc_ref):
    e = pl.program_id(1)

    @pl.when(e == 0)
    def _():
        acc_ref[...] = jnp.zeros_like(acc_ref)

    s = lax.dot_general(h_ref[...], u_ref[...], (((1,), (1,)), ((), ())), preferred_element_type=F32)
    act = 0.5 * s * (1.0 + lax.erf(s * math.sqrt(0.5)))
    a = (g_ref[...].astype(F32) * act).astype(BF16)
    acc_ref[...] += jnp.dot(a, v_ref[...], preferred_element_type=F32)

    @pl.when(e == pl.num_programs(1) - 1)
    def _():
        o_ref[...] = y_ref[...] + acc_ref[...]


def _experts(h2d, gdense, u_tab, v_tab, y2d, *, tile, eblk):
    n = h2d.shape[0]
    return pl.pallas_call(
        _experts_kernel, grid=(n // tile, N_EXPERTS // eblk),
        in_specs=[pl.BlockSpec((tile, D_MODEL), lambda i, e: (i, 0)),
                  pl.BlockSpec((tile, eblk), lambda i, e: (i, e)),
                  pl.BlockSpec((eblk, D_MODEL), lambda i, e: (e, 0)),
                  pl.BlockSpec((eblk, D_MODEL), lambda i, e: (e, 0)),
                  pl.BlockSpec((tile, D_MODEL), lambda i, e: (i, 0))],
        out_specs=pl.BlockSpec((tile, D_MODEL), lambda i, e: (i, 0)),
        out_shape=jax.ShapeDtypeStruct((n, D_MODEL), F32),
        scratch_shapes=[pltpu.VMEM((tile, D_MODEL), F32)],
        compiler_params=_cparams(("arbitrary", "arbitrary")), name="experts",
    )(h2d, gdense, u_tab, v_tab, y2d)


PROJ_TILE = 512
ATTN_TILE = 512
CONV_TILE = 256
ROUTE_TILE = 128
GATES_TILE = 64
EXPERT_TILE = 1024
EXPERT_BLOCK = 512
DECODE_PAGES_PER_STEP = 8


def _finish(x2d, a2d, c2d, route_wts, u_tab, v_tab, *, expert_tile):
    n = x2d.shape[0]
    y, h2, i1, i2, g = _route(x2d, a2d, c2d, route_wts, tile=min(ROUTE_TILE, n))
    gd = _gates(i1, i2, g, tile=GATES_TILE).reshape(n, N_EXPERTS)
    return _experts(h2, gd, u_tab, v_tab, y, tile=expert_tile, eblk=EXPERT_BLOCK)


def kernel(x_prompt, x_sample, cache_k, cache_v, cache_logf, state_conv, page_table, norm1_g, w_in, b_forget,
           q_gain, k_gain, conv_w, conv_b, conv_ln_g, conv_ln_b, w_out, norm2_g, peer_wq, peer_subkeys,
           peer_u, peer_v):
    assert w_in.shape[0] == 1, "single layer"
    batch, seq_len, _ = x_prompt.shape
    db = x_sample.shape[0]
    n_pool, page = cache_k.shape[1], cache_k.shape[2]

    w = w_in[0].astype(BF16)
    w_qkv = w[:, :3 * ATTN_W]
    w_ag = w[:, 3 * ATTN_W:3 * ATTN_W + 2 * CONV_CH]
    w_fg = jnp.pad(w[:, 3 * ATTN_W + 2 * CONV_CH:], ((0, 0), (0, LANES - N_HEADS)))
    b_fg = jnp.pad(b_forget[0].reshape(1, N_HEADS), ((0, 0), (0, LANES - N_HEADS)))
    qg = jnp.tile(q_gain[0], N_HEADS).reshape(1, ATTN_W)
    kg = jnp.tile(k_gain[0], N_HEADS).reshape(1, ATTN_W)
    gi = lax.broadcasted_iota(jnp.int32, (ATTN_W, ATTN_W), 0) // HEAD_DIM
    gj = lax.broadcasted_iota(jnp.int32, (ATTN_W, ATTN_W), 1) // HEAD_DIM
    bd = (gi == gj).astype(BF16)
    tri = (lax.broadcasted_iota(jnp.int32, (LANES, LANES), 1) <=
           lax.broadcasted_iota(jnp.int32, (LANES, LANES), 0)).astype(BF16)
    proj_wts = (norm1_g[0].reshape(1, D_MODEL), w_qkv, w_ag, w_fg, b_fg, qg, kg, bd, tri)
    conv_wts = (conv_w[0], conv_b[0].reshape(1, CONV_CH), conv_ln_g[0].reshape(1, CONV_CH),
                conv_ln_b[0].reshape(1, CONV_CH))
    wo = w_out[0].astype(BF16)
    sk = peer_subkeys[0].astype(BF16).reshape(2 * PEER_HEADS, PEER_KEYS, PEER_KEYS)
    route_wts = (wo[:ATTN_W], wo[ATTN_W:], norm2_g[0].reshape(1, D_MODEL), peer_wq[0].astype(BF16), sk)
    u_tab = peer_u[0].astype(BF16)
    v_tab = peer_v[0].astype(BF16)

    xp2 = x_prompt.reshape(batch * seq_len, D_MODEL)
    k_p, v_p, u_p, lf_p, qb, kb, vb, ccol, crow = _proj(
        xp2, proj_wts, tile=PROJ_TILE, seq_len=seq_len, with_cumsum=True)
    a_p = _attn(qb, kb, vb, ccol, crow, batch=batch, seq_len=seq_len, tq=ATTN_TILE)
    c_p = _conv_prompt(u_p, *conv_wts, tile=CONV_TILE, seq_len=seq_len)
    y_p = _finish(xp2, a_p, c_p, route_wts, u_tab, v_tab, expert_tile=EXPERT_TILE)

    xs2 = x_sample.reshape(db, D_MODEL)
    k_s, v_s, u_s, lf_s, qsb, ksb, vsb, _, _ = _proj(xs2, proj_wts, tile=db, seq_len=db, with_cumsum=False)
    lf_t = cache_logf[0].transpose(0, 2, 1)
    a_s = _decode(page_table, qsb.reshape(db, 1, ATTN_W), ksb.reshape(db, 1, ATTN_W), vsb.reshape(db, 1, ATTN_W),
                  lf_s.reshape(db, N_HEADS, 1),
                  cache_k[0].reshape(n_pool, page, ATTN_W), cache_v[0].reshape(n_pool, page, ATTN_W), lf_t,
                  pages_per_step=DECODE_PAGES_PER_STEP).reshape(db, ATTN_W)
    window = jnp.concatenate([state_conv[0], u_s[:, None, :]], axis=1)
    c_s = _conv_step(window.transpose(1, 0, 2), *conv_wts)
    y_s = _finish(xs2, a_s, c_s, route_wts, u_tab, v_tab, expert_tile=db)

    tail = CONV_W - 1
    return (y_p.reshape(batch, seq_len, D_MODEL),
            y_s.reshape(db, 1, D_MODEL),
            k_p.reshape(1, batch, seq_len, N_HEADS, HEAD_DIM),
            v_p.reshape(1, batch, seq_len, N_HEADS, HEAD_DIM),
            lf_p.reshape(1, batch, seq_len, N_HEADS),
            u_p.reshape(batch, seq_len, CONV_CH)[:, seq_len - tail:][None],
            k_s.reshape(1, db, 1, N_HEADS, HEAD_DIM),
            v_s.reshape(1, db, 1, N_HEADS, HEAD_DIM),
            lf_s.reshape(1, db, 1, N_HEADS),
            window[:, 1:][None])
```

```python
import functools
import math

import jax
import jax.numpy as jnp
from jax import lax
from jax.experimental import pallas as pl
from jax.experimental.pallas import tpu as pltpu

F32 = jnp.float32
BF16 = jnp.bfloat16

D_MODEL = 1024
HEAD_DIM = 64
N_HEADS = 8
ATTN_W = N_HEADS * HEAD_DIM
CONV_CH = D_MODEL - ATTN_W
CONV_W = 31
PEER_HEADS = 8
PEER_KEYS = 128
PEER_TOPK = 16
N_EXPERTS = PEER_KEYS * PEER_KEYS
EPS = 1e-6
LANES = 128
SUBLANES = 8
MXU_TILE = 256
VMEM_LIMIT = 56 * 1024 * 1024
NEG_INF = float("-inf")


def _cparams(sem):
    return pltpu.CompilerParams(dimension_semantics=sem, vmem_limit_bytes=VMEM_LIMIT)


def _split3(x):
    p1 = x.astype(BF16)
    r1 = x - p1.astype(F32)
    p2 = r1.astype(BF16)
    p3 = (r1 - p2.astype(F32)).astype(BF16)
    return p1, p2, p3


def _full_spec(shape):
    return pl.BlockSpec(shape, lambda *_: (0,) * len(shape))


def _group_rms(t, bd, gain):
    sq = t * t
    hi = sq.astype(BF16)
    lo = (sq - hi.astype(F32)).astype(BF16)
    ss = jnp.dot(hi, bd, preferred_element_type=F32) + jnp.dot(lo, bd, preferred_element_type=F32)
    return t * lax.rsqrt(ss * (1.0 / HEAD_DIM) + EPS) * gain


def _proj_kernel(x_ref, g1_ref, wqkv_ref, wag_ref, wfg_ref, bf_ref, qg_ref, kg_ref, bd_ref, tri_ref, pk_ref, pc_ref,
                 k_ref, v_ref, u_ref, lf_ref, qb_ref, kb_ref, vb_ref, kaug_ref, vt_ref,
                 carry_ref, *, tiles_per_seq, with_cumsum):
    t = x_ref.shape[0]
    x = x_ref[...]
    ms = jnp.mean(x * x, axis=-1, keepdims=True)
    h = (x * lax.rsqrt(ms + EPS) * g1_ref[...]).astype(BF16)

    qkv = jnp.dot(h, wqkv_ref[...], preferred_element_type=F32)
    bd = bd_ref[...]
    q = _group_rms(qkv[:, :ATTN_W], bd, qg_ref[...])
    k = _group_rms(qkv[:, ATTN_W:2 * ATTN_W], bd, kg_ref[...])
    v = qkv[:, 2 * ATTN_W:]
    k_ref[...] = k
    v_ref[...] = v
    qb_ref[...] = (q * (HEAD_DIM ** -0.5)).astype(BF16)
    kb_ref[...] = k.astype(BF16)
    vb_ref[...] = v.astype(BF16)

    ag = jnp.dot(h, wag_ref[...], preferred_element_type=F32)
    u_ref[...] = ag[:, :CONV_CH] * jax.nn.sigmoid(ag[:, CONV_CH:])

    z = jnp.dot(h, wfg_ref[...], preferred_element_type=F32) + bf_ref[...]
    logf = jnp.minimum(z, 0.0) - jnp.log1p(jnp.exp(-jnp.abs(z)))
    lf_ref[...] = logf[:, :N_HEADS]

    if with_cumsum:
        vt_ref[...] = v.T.astype(BF16)
        i = pl.program_id(0)
        lane = lax.broadcasted_iota(jnp.int32, (1, LANES), 1)
        carry = jnp.where(i % tiles_per_seq == 0, 0.0, carry_ref[...])
        tri = tri_ref[...]
        for sb in range(t // LANES):
            rows = slice(sb * LANES, (sb + 1) * LANES)
            blk = jnp.where(lane < N_HEADS, logf[rows], 0.0)
            p1, p2, p3 = _split3(blk)
            cs = (jnp.dot(tri, p1, preferred_element_type=F32)
                  + jnp.dot(tri, p2, preferred_element_type=F32)
                  + jnp.dot(tri, p3, preferred_element_type=F32)) + carry
            carry = cs[LANES - 1:LANES, :]
            placed = (jnp.dot(k[rows].astype(BF16), pk_ref[...], preferred_element_type=F32)
                      + jnp.dot(jnp.concatenate(_split3(-cs), axis=1), pc_ref[...], preferred_element_type=F32))
            kaug_ref[rows, :] = placed.astype(BF16)
        carry_ref[...] = carry
    else:
        kaug_ref[...] = jnp.zeros_like(kaug_ref)
        vt_ref[...] = jnp.zeros_like(vt_ref)


def _proj(x2d, wts, *, tile, seq_len, with_cumsum):
    n = x2d.shape[0]
    nt = n // tile
    row = lambda w: pl.BlockSpec((tile, w), lambda i: (i, 0))
    kern = functools.partial(_proj_kernel, tiles_per_seq=max(seq_len // tile, 1), with_cumsum=with_cumsum)
    out_shape = (
        jax.ShapeDtypeStruct((n, ATTN_W), F32),
        jax.ShapeDtypeStruct((n, ATTN_W), F32),
        jax.ShapeDtypeStruct((n, CONV_CH), F32),
        jax.ShapeDtypeStruct((n, N_HEADS), F32),
        jax.ShapeDtypeStruct((n, ATTN_W), BF16),
        jax.ShapeDtypeStruct((n, ATTN_W), BF16),
        jax.ShapeDtypeStruct((n, ATTN_W), BF16),
        jax.ShapeDtypeStruct((n, N_HEADS * LANES), BF16),
        jax.ShapeDtypeStruct((ATTN_W, n), BF16),
    )
    out_specs = (row(ATTN_W), row(ATTN_W), row(CONV_CH), row(N_HEADS), row(ATTN_W), row(ATTN_W), row(ATTN_W),
                 row(N_HEADS * LANES), pl.BlockSpec((ATTN_W, tile), lambda i: (0, i)))
    in_specs = [row(D_MODEL)] + [_full_spec(w.shape) for w in wts]
    return pl.pallas_call(
        kern, grid=(nt,), in_specs=in_specs, out_specs=out_specs, out_shape=out_shape,
        scratch_shapes=[pltpu.VMEM((1, LANES), F32)],
        compiler_params=_cparams(("arbitrary",)), name="proj",
    )(x2d, *wts)


N_BIAS_TERMS = 3


def _attn_kernel(q_ref, kaug_ref, vt_ref, o_ref, acc_ref, *, tq, tk):
    qi = pl.program_id(2)
    lane = lax.broadcasted_iota(jnp.int32, (1, LANES), 1)
    q = q_ref[...].astype(F32)
    ones = jnp.where(lane < HEAD_DIM + N_BIAS_TERMS, 1.0, 0.0)
    key_row = lax.broadcasted_iota(jnp.int32, (tk, tq), 0)
    qry_col = lax.broadcasted_iota(jnp.int32, (tk, tq), 1)
    last = (qi * tq) // tk
    visible = key_row + last * tk <= qry_col + qi * tq
    qhs = []
    for hh in range(2):
        qh = q if hh == 0 else pltpu.roll(q, HEAD_DIM, axis=1)
        qhs.append(jnp.where(lane < HEAD_DIM, qh, ones).astype(BF16))
    acc_ref[...] = jnp.zeros_like(acc_ref)

    def scores(j):
        start = pl.multiple_of(j * tk, tk)
        return tuple(
            lax.dot_general(kaug_ref[pl.ds(start, tk), hh * LANES:(hh + 1) * LANES], qhs[hh],
                            (((1,), (1,)), ((), ())), preferred_element_type=F32)
            for hh in range(2))

    def consume(j, ss, stats, masked):
        start = pl.multiple_of(j * tk, tk)
        vt = vt_ref[:, pl.ds(start, tk)]
        new_stats = []
        for hh in range(2):
            m, l = stats[hh]
            s = jnp.where(visible, ss[hh], NEG_INF) if masked else ss[hh]
            m_new = jnp.maximum(m, jnp.max(s, axis=0, keepdims=True))
            alpha = jnp.exp(m - m_new)
            pr = jnp.exp(s - m_new)
            l = alpha * l + jnp.sum(pr, axis=0, keepdims=True)
            acc_ref[hh] = alpha * acc_ref[hh] + jnp.dot(vt, pr.astype(BF16), preferred_element_type=F32)
            new_stats.append((m_new, l))
        return tuple(new_stats)

    def body(j, carry):
        ss, stats = carry
        nxt = scores(j + 1)
        return nxt, consume(j, ss, stats, False)

    init = (jnp.full((1, tq), NEG_INF, F32), jnp.zeros((1, tq), F32))
    ss, stats = lax.fori_loop(0, last, body, (scores(0), (init, init)))
    stats = consume(last, ss, stats, True)
    row = lax.broadcasted_iota(jnp.int32, (LANES, 1), 0)
    out = jnp.where(row < HEAD_DIM, acc_ref[0] / stats[0][1], acc_ref[1] / stats[1][1])
    o_ref[...] = out.T.astype(o_ref.dtype)


def _attn(qb, kaug, vt, *, batch, seq_len, tq, tk):
    n = qb.shape[0]
    nq = seq_len // tq
    kern = functools.partial(_attn_kernel, tq=tq, tk=tk)
    return pl.pallas_call(
        kern, grid=(batch, N_HEADS // 2, nq),
        in_specs=[
            pl.BlockSpec((tq, LANES), lambda b, p, i: (b * nq + i, p)),
            pl.BlockSpec((seq_len, 2 * LANES), lambda b, p, i: (b, p)),
            pl.BlockSpec((LANES, seq_len), lambda b, p, i: (p, b)),
        ],
        out_specs=pl.BlockSpec((tq, LANES), lambda b, p, i: (b * nq + i, p)),
        out_shape=jax.ShapeDtypeStruct((n, ATTN_W), BF16),
        scratch_shapes=[pltpu.VMEM((2, LANES, tq), F32)],
        compiler_params=_cparams(("arbitrary", "arbitrary", "arbitrary")), name="attn",
    )(qb, kaug, vt)


def _decode_kernel(pt_ref, q_ref, kn_ref, vn_ref, lfn_ref, su_ref, *rest, pages_per_step):
    del pt_ref
    pp = pages_per_step
    k_refs = rest[:pp]
    v_refs = rest[pp:2 * pp]
    lf_refs = rest[2 * pp:3 * pp]
    o_ref, m_ref, l_ref, acc_ref, carry_ref = rest[3 * pp:]
    j = pl.program_id(1)
    nj = pl.num_programs(1)
    page = lf_refs[0].shape[1]

    sub = lax.broadcasted_iota(jnp.int32, (N_HEADS, ATTN_W), 0)
    lane = lax.broadcasted_iota(jnp.int32, (N_HEADS, ATTN_W), 1)
    headmask = (lane // HEAD_DIM) == sub
    qbd = jnp.where(headmask, jnp.broadcast_to(q_ref[...].astype(F32), (N_HEADS, ATTN_W)), 0.0)
    qbd = qbd.astype(BF16)

    @pl.when(j == 0)
    def _():
        m_ref[...] = jnp.full_like(m_ref, NEG_INF)
        l_ref[...] = jnp.zeros_like(l_ref)
        acc_ref[...] = jnp.zeros_like(acc_ref)
        carry_ref[...] = jnp.broadcast_to(lfn_ref[...], carry_ref.shape)

    parts = []
    for r in lf_refs:
        parts.extend(_split3(r[...]))
    sums = jnp.dot(jnp.concatenate(parts, axis=0), su_ref[...], preferred_element_type=F32)
    carry = carry_ref[...]
    m = m_ref[...]
    l = l_ref[...]
    for i in range(pp):
        base = 3 * N_HEADS * i
        both = sums[base:base + 8] + sums[base + 8:base + 16] + sums[base + 16:base + 24]
        s = jnp.dot(qbd, k_refs[i][...].astype(BF16), preferred_element_type=F32)
        s = s + both[:, :page] + carry
        carry = carry + both[:, page:]
        m_new = jnp.maximum(m, s)
        alpha = jnp.exp(m - m_new)
        pr = jnp.exp(s - m_new)
        l = alpha * l + pr
        m = m_new
        for h in range(N_HEADS):
            rows = slice(h * HEAD_DIM, (h + 1) * HEAD_DIM)
            acc_ref[rows, :] = acc_ref[rows, :] * alpha[h:h + 1, :] + v_refs[i][rows, :] * pr[h:h + 1, :]
    carry_ref[...] = carry
    m_ref[...] = m
    l_ref[...] = l

    @pl.when(j == nj - 1)
    def _():
        s_self = jnp.sum(qbd.astype(F32) * kn_ref[...].astype(F32), axis=-1, keepdims=True)
        m_all = jnp.maximum(jnp.max(m, axis=-1, keepdims=True), s_self)
        w = jnp.exp(m - m_all)
        w_self = jnp.exp(s_self - m_all)
        denom = jnp.sum(l * w, axis=-1, keepdims=True) + w_self
        ones = jnp.ones((N_HEADS, page), BF16)
        o = jnp.zeros((N_HEADS, ATTN_W), F32)
        weighted = jnp.concatenate(
            [acc_ref[h * HEAD_DIM:(h + 1) * HEAD_DIM, :] * w[h:h + 1, :] for h in range(N_HEADS)], axis=0)
        for part in _split3(weighted):
            o = o + lax.dot_general(ones, part, (((1,), (1,)), ((), ())), preferred_element_type=F32)
        o = o + w_self * jnp.broadcast_to(vn_ref[...].astype(F32), (N_HEADS, ATTN_W))
        o = jnp.where(headmask, o / denom, 0.0)
        o_ref[...] = jnp.sum(o, axis=0, keepdims=True).astype(o_ref.dtype)


def _decode(page_table, qb, kb, vb, lf_new, cache_kt, cache_vt, cache_lf_t, *, pages_per_step):
    db, n_pages = page_table.shape
    pp = pages_per_step
    nj = n_pages // pp
    page = cache_lf_t.shape[2]
    newer = (lax.broadcasted_iota(jnp.int32, (page, page), 0) >
             lax.broadcasted_iota(jnp.int32, (page, page), 1))
    su = jnp.concatenate([newer.astype(BF16), jnp.ones((page, page), BF16)], axis=1)

    def page_idx(i):
        return lambda b, j, pt: (pt[b, n_pages - 1 - (j * pp + i)], 0, 0)

    per_seq = lambda: pl.BlockSpec((None, 1, ATTN_W), lambda b, j, pt: (b, 0, 0))
    in_specs = [per_seq(), per_seq(), per_seq(),
                pl.BlockSpec((None, N_HEADS, 1), lambda b, j, pt: (b, 0, 0)),
                pl.BlockSpec((page, 2 * page), lambda b, j, pt: (0, 0))]
    in_specs += [pl.BlockSpec((None, ATTN_W, page), page_idx(i)) for i in range(pp)]
    in_specs += [pl.BlockSpec((None, ATTN_W, page), page_idx(i)) for i in range(pp)]
    in_specs += [pl.BlockSpec((None, N_HEADS, page), page_idx(i)) for i in range(pp)]
    grid_spec = pltpu.PrefetchScalarGridSpec(
        num_scalar_prefetch=1, grid=(db, nj), in_specs=in_specs,
        out_specs=per_seq(),
        scratch_shapes=[pltpu.VMEM((N_HEADS, page), F32), pltpu.VMEM((N_HEADS, page), F32),
                        pltpu.VMEM((ATTN_W, page), F32), pltpu.VMEM((N_HEADS, page), F32)])
    kern = functools.partial(_decode_kernel, pages_per_step=pp)
    return pl.pallas_call(
        kern, grid_spec=grid_spec, out_shape=jax.ShapeDtypeStruct((db, 1, ATTN_W), BF16),
        compiler_params=_cparams(("arbitrary", "arbitrary")), name="decode",
    )(page_table, qb, kb, vb, lf_new, su, *([cache_kt] * pp), *([cache_vt] * pp), *([cache_lf_t] * pp))


CONV_HALO = 32
CONV_ROWS = 32


def _ln_silu(y, g, b):
    mu = jnp.mean(y, axis=-1, keepdims=True)
    d = y - mu
    var = jnp.mean(d * d, axis=-1, keepdims=True)
    z = d * lax.rsqrt(var + EPS) * g + b
    return z * jax.nn.sigmoid(z)


def _conv_kernel(cur_ref, prev_ref, w_ref, b_ref, g_ref, lb_ref, o_ref, xp_ref, *, tiles_per_seq):
    t = cur_ref.shape[0]
    i = pl.program_id(0)
    xp_ref[0:CONV_HALO, :] = jnp.where(i % tiles_per_seq == 0, 0.0, prev_ref[...])
    xp_ref[CONV_HALO:, :] = cur_ref[...]
    off = CONV_HALO - (CONV_W - 1)
    for c in range(t // CONV_ROWS):
        acc = jnp.broadcast_to(b_ref[...], (CONV_ROWS, CONV_CH))
        for w in range(CONV_W):
            acc = acc + xp_ref[pl.ds(c * CONV_ROWS + off + w, CONV_ROWS), :] * w_ref[w:w + 1, :]
        o_ref[c * CONV_ROWS:(c + 1) * CONV_ROWS, :] = _ln_silu(acc, g_ref[...], lb_ref[...]).astype(o_ref.dtype)


def _conv_prompt(u2d, conv_w, conv_b, ln_g, ln_b, *, tile, seq_len):
    n = u2d.shape[0]
    ratio = tile // CONV_HALO
    kern = functools.partial(_conv_kernel, tiles_per_seq=seq_len // tile)
    wts = (conv_w, conv_b, ln_g, ln_b)
    return pl.pallas_call(
        kern, grid=(n // tile,),
        in_specs=[pl.BlockSpec((tile, CONV_CH), lambda i: (i, 0)),
                  pl.BlockSpec((CONV_HALO, CONV_CH), lambda i: (jnp.maximum(i * ratio - 1, 0), 0))]
                 + [_full_spec(w.shape) for w in wts],
        out_specs=pl.BlockSpec((tile, CONV_CH), lambda i: (i, 0)),
        out_shape=jax.ShapeDtypeStruct((n, CONV_CH), BF16),
        scratch_shapes=[pltpu.VMEM((tile + CONV_HALO, CONV_CH), F32)],
        compiler_params=_cparams(("arbitrary",)), name="conv_prompt",
    )(u2d, u2d, *wts)


def _conv_step_kernel(xp_ref, w_ref, b_ref, g_ref, lb_ref, o_ref):
    acc = jnp.broadcast_to(b_ref[...], o_ref.shape)
    for w in range(CONV_W):
        acc = acc + xp_ref[w] * w_ref[w:w + 1, :]
    o_ref[...] = _ln_silu(acc, g_ref[...], lb_ref[...]).astype(o_ref.dtype)


def _conv_step(xp_t, conv_w, conv_b, ln_g, ln_b):
    db = xp_t.shape[1]
    args = (xp_t, conv_w, conv_b, ln_g, ln_b)
    return pl.pallas_call(
        _conv_step_kernel, grid=(1,),
        in_specs=[_full_spec(a.shape) for a in args],
        out_specs=_full_spec((db, CONV_CH)),
        out_shape=jax.ShapeDtypeStruct((db, CONV_CH), BF16),
        compiler_params=_cparams(("arbitrary",)), name="conv_step",
    )(*args)


N_CAND = 64
ROUTE_CHAINS = 2


def _top16(chains, iota_f):
    ss = [c[0] for c in chains]
    for r in range(PEER_TOPK):
        for n, (_, val_ref, idx_ref) in enumerate(chains):
            m = jnp.max(ss[n], axis=0, keepdims=True)
            idx = jnp.min(jnp.where(ss[n] == m, iota_f, float(1 << 20)), axis=0, keepdims=True)
            val_ref[r:r + 1, :] = m
            idx_ref[r:r + 1, :] = idx
            ss[n] = jnp.where(iota_f == idx, NEG_INF, ss[n])


def _candidate_ids(t):
    row = lax.broadcasted_iota(jnp.int32, (N_CAND, t), 0)
    r1 = jnp.where(row < 16, 0, jnp.where(row < 40, (row - 8) // 8, jnp.where(row < 56, (row - 24) // 4, row - 48)))
    r2 = jnp.where(row < 16, row, jnp.where(row < 40, row % 8, jnp.where(row < 56, row % 4, 0)))
    return (r1 * PEER_TOPK + r2).astype(F32)


def _route_kernel(x_ref, a_ref, c_ref, woa_ref, woc_ref, g2_ref, wq_ref, sk_ref,
                  y_ref, h_ref, i1_ref, i2_ref, g_ref,
                  sc_ref, tv_ref, ti_ref, cand_ref, bs_ref, o1_ref, o2_ref, og_ref):
    t = x_ref.shape[0]
    y = (x_ref[...] + jnp.dot(a_ref[...], woa_ref[...], preferred_element_type=F32)
         + jnp.dot(c_ref[...], woc_ref[...], preferred_element_type=F32))
    y_ref[...] = y
    ms = jnp.mean(y * y, axis=-1, keepdims=True)
    h = (y * lax.rsqrt(ms + EPS) * g2_ref[...]).astype(BF16)
    h_ref[...] = h
    q = jnp.dot(h, wq_ref[...], preferred_element_type=F32).astype(BF16)
    for hp in range(2 * PEER_HEADS):
        sc_ref[hp] = lax.dot_general(sk_ref[hp], q[:, hp * PEER_KEYS:(hp + 1) * PEER_KEYS],
                                     (((1,), (1,)), ((), ())), preferred_element_type=F32)

    key_iota = lax.broadcasted_iota(jnp.int32, (PEER_KEYS, t), 0).astype(F32)

    def stage1(it, c):
        hps = [it * ROUTE_CHAINS + n for n in range(ROUTE_CHAINS)]
        _top16([(sc_ref[hp], tv_ref.at[hp], ti_ref.at[hp]) for hp in hps], key_iota)
        return c

    lax.fori_loop(0, 2 * PEER_HEADS // ROUTE_CHAINS, stage1, 0)

    cand_id = _candidate_ids(t)
    rank_iota = lax.broadcasted_iota(jnp.int32, (PEER_TOPK, t), 0).astype(F32)
    low4 = lax.broadcasted_iota(jnp.int32, (SUBLANES, t), 0) < 4

    def stage2(it, c):
        heads = [it * ROUTE_CHAINS + n for n in range(ROUTE_CHAINS)]
        ss, i1s, i2s = [], [], []
        for n, hd in enumerate(heads):
            v1 = tv_ref[2 * hd]
            v2 = tv_ref[2 * hd + 1]
            i1s.append(ti_ref[2 * hd])
            i2s.append(ti_ref[2 * hd + 1])
            v2lo = v2[0:SUBLANES, :]
            v2q = jnp.where(low4, v2lo, pltpu.roll(v2lo, 4, axis=0))
            cand_ref[n, 0:16, :] = v1[0:1, :] + v2
            for r1 in (1, 2, 3):
                cand_ref[n, 8 + 8 * r1:16 + 8 * r1, :] = v1[r1:r1 + 1, :] + v2lo
            cand_ref[n, 40:48, :] = jnp.where(low4, v1[4:5, :], v1[5:6, :]) + v2q
            cand_ref[n, 48:56, :] = jnp.where(low4, v1[6:7, :], v1[7:8, :]) + v2q
            cand_ref[n, 56:64, :] = v1[8:16, :] + v2[0:1, :]
            ss.append(cand_ref[n])
        for r in range(PEER_TOPK):
            for n, hd in enumerate(heads):
                base = pl.multiple_of(hd * PEER_TOPK, PEER_TOPK)
                m = jnp.max(ss[n], axis=0, keepdims=True)
                cid = jnp.min(jnp.where(ss[n] == m, cand_id, float(1 << 20)), axis=0, keepdims=True)
                ss[n] = jnp.where(cand_id == cid, NEG_INF, ss[n])
                r1 = jnp.floor(cid * (1.0 / PEER_TOPK))
                r2 = cid - r1 * PEER_TOPK
                bs_ref[n, r:r + 1, :] = m
                o1_ref[pl.ds(base + r, 1), :] = jnp.max(jnp.where(rank_iota == r1, i1s[n], -1.0), axis=0, keepdims=True)
                o2_ref[pl.ds(base + r, 1), :] = jnp.max(jnp.where(rank_iota == r2, i2s[n], -1.0), axis=0, keepdims=True)
        for n, hd in enumerate(heads):
            base = pl.multiple_of(hd * PEER_TOPK, PEER_TOPK)
            bs = bs_ref[n]
            e = jnp.exp(bs - bs[0:1, :])
            og_ref[pl.ds(base, PEER_TOPK), :] = e / jnp.sum(e, axis=0, keepdims=True)
        return c

    lax.fori_loop(0, PEER_HEADS // ROUTE_CHAINS, stage2, 0)
    i1_ref[...] = o1_ref[...].T
    i2_ref[...] = o2_ref[...].T
    g_ref[...] = og_ref[...].T


def _route(x2d, a2d, c2d, wts, *, tile):
    n = x2d.shape[0]
    row = lambda w: pl.BlockSpec((tile, w), lambda i: (i, 0))
    hk = PEER_HEADS * PEER_TOPK
    out_shape = (jax.ShapeDtypeStruct((n, D_MODEL), F32),
                 jax.ShapeDtypeStruct((n, D_MODEL), BF16),
                 jax.ShapeDtypeStruct((n, hk), F32),
                 jax.ShapeDtypeStruct((n, hk), F32),
                 jax.ShapeDtypeStruct((n, hk), F32))
    return pl.pallas_call(
        _route_kernel, grid=(n // tile,),
        in_specs=[row(D_MODEL), row(ATTN_W), row(CONV_CH)] + [_full_spec(w.shape) for w in wts],
        out_specs=(row(D_MODEL), row(D_MODEL), row(hk), row(hk), row(hk)),
        out_shape=out_shape,
        scratch_shapes=[pltpu.VMEM((2 * PEER_HEADS, PEER_KEYS, tile), F32),
                        pltpu.VMEM((2 * PEER_HEADS, PEER_TOPK, tile), F32),
                        pltpu.VMEM((2 * PEER_HEADS, PEER_TOPK, tile), F32),
                        pltpu.VMEM((ROUTE_CHAINS, N_CAND, tile), F32),
                        pltpu.VMEM((ROUTE_CHAINS, PEER_TOPK, tile), F32),
                        pltpu.VMEM((hk, tile), F32), pltpu.VMEM((hk, tile), F32), pltpu.VMEM((hk, tile), F32)],
        compiler_params=_cparams(("arbitrary",)), name="route",
    )(x2d, a2d, c2d, *wts)


GATE_PITCH = PEER_KEYS + SUBLANES
GATE_UNROLL = 8


def _gates_kernel(i1_ref, i2_ref, g_ref, o_ref, stage_ref):
    t = i1_ref.shape[0]
    hk = i1_ref.shape[1]
    sub = lax.broadcasted_iota(jnp.int32, (PEER_KEYS, hk), 0).astype(F32)

    def body(tb, c):
        for u in range(GATE_UNROLL):
            tok = tb * GATE_UNROLL + u
            i1 = i1_ref[pl.ds(tok, 1), :]
            i2 = i2_ref[pl.ds(tok, 1), :]
            g = g_ref[pl.ds(tok, 1), :]
            p1 = jnp.where(sub == i1, g, 0.0).astype(BF16)
            p2 = jnp.where(sub == i2, 1.0, 0.0).astype(BF16)
            grid = lax.dot_general(p1, p2, (((1,), (1,)), ((), ())), preferred_element_type=F32)
            stage_ref[pl.ds(pl.multiple_of(tok * GATE_PITCH, SUBLANES), PEER_KEYS), :] = grid
        return c

    lax.fori_loop(0, t // GATE_UNROLL, body, 0)

    def gather(a, c):
        rows = stage_ref[pl.ds(a, t, stride=GATE_PITCH), :]
        o_ref[:, pl.ds(pl.multiple_of(a * PEER_KEYS, PEER_KEYS), PEER_KEYS)] = rows.astype(o_ref.dtype)
        return c

    lax.fori_loop(0, PEER_KEYS, gather, 0, unroll=4)


def _gates(i1, i2, g, *, tile):
    n, hk = i1.shape
    row = pl.BlockSpec((tile, hk), lambda i: (i, 0))
    return pl.pallas_call(
        _gates_kernel, grid=(n // tile,), in_specs=[row, row, row],
        out_specs=pl.BlockSpec((tile, N_EXPERTS), lambda i: (i, 0)),
        out_shape=jax.ShapeDtypeStruct((n, N_EXPERTS), BF16),
        scratch_shapes=[pltpu.VMEM((tile * GATE_PITCH, PEER_KEYS), F32)],
        compiler_params=_cparams(("arbitrary",)), name="gates",
    )(i1, i2, g)


def _experts_kernel(h_ref, g_ref, u_ref, v_ref, y_ref, o_ref, acc_ref, even_ref, odd_ref):
    e = pl.program_id(1)

    @pl.when(e == 0)
    def _():
        acc_ref[...] = jnp.zeros_like(acc_ref)
        odd_ref[...] = jnp.zeros_like(odd_ref)

    def work(cur_ref, prev_ref):
        h = h_ref[...]
        for c in range(u_ref.shape[0] // MXU_TILE):
            cols = slice(c * MXU_TILE, (c + 1) * MXU_TILE)
            s = lax.dot_general(h, u_ref[cols, :], (((1,), (1,)), ((), ())), preferred_element_type=F32)
            act = 0.5 * s * (1.0 + lax.erf(s * math.sqrt(0.5)))
            cur_ref[:, cols] = (g_ref[:, cols].astype(F32) * act).astype(BF16)
        prev = prev_ref[...]
        for c in range(v_ref.shape[1] // MXU_TILE):
            cols = slice(c * MXU_TILE, (c + 1) * MXU_TILE)
            acc_ref[:, cols] += jnp.dot(prev, v_ref[:, cols], preferred_element_type=F32)

    pl.when(e % 2 == 0)(lambda: work(even_ref, odd_ref))
    pl.when(e % 2 == 1)(lambda: work(odd_ref, even_ref))

    @pl.when(e == pl.num_programs(1) - 1)
    def _():
        o_ref[...] = y_ref[...] + acc_ref[...]


def _experts(h2d, gdense, u_tab, v_tab, y2d, *, tile, eblk):
    n = h2d.shape[0]
    nb = N_EXPERTS // eblk
    scored = lambda e: jnp.minimum(e, nb - 1)
    folded = lambda e: jnp.maximum(e - 1, 0)
    return pl.pallas_call(
        _experts_kernel, grid=(n // tile, nb + 1),
        in_specs=[pl.BlockSpec((tile, D_MODEL), lambda i, e: (i, 0)),
                  pl.BlockSpec((tile, eblk), lambda i, e: (i, scored(e))),
                  pl.BlockSpec((eblk, D_MODEL), lambda i, e: (scored(e), 0)),
                  pl.BlockSpec((eblk, D_MODEL), lambda i, e: (folded(e), 0)),
                  pl.BlockSpec((tile, D_MODEL), lambda i, e: (i, 0))],
        out_specs=pl.BlockSpec((tile, D_MODEL), lambda i, e: (i, 0)),
        out_shape=jax.ShapeDtypeStruct((n, D_MODEL), F32),
        scratch_shapes=[pltpu.VMEM((tile, D_MODEL), F32), pltpu.VMEM((tile, eblk), BF16),
                        pltpu.VMEM((tile, eblk), BF16)],
        compiler_params=_cparams(("arbitrary", "arbitrary")), name="experts",
    )(h2d, gdense, u_tab, v_tab, y2d)


PROJ_TILE = 512
ATTN_TILE = 256
ATTN_KEY_TILE = 512
CONV_TILE = 256
ROUTE_TILE = 128
GATES_TILE = 64
EXPERT_TILE = 1024
EXPERT_BLOCK = 512
DECODE_PAGES_PER_STEP = 8


def _finish(x2d, a2d, c2d, route_wts, u_tab, v_tab, *, expert_tile):
    n = x2d.shape[0]
    y, h2, i1, i2, g = _route(x2d, a2d, c2d, route_wts, tile=min(ROUTE_TILE, n))
    gd = _gates(i1, i2, g, tile=GATES_TILE)
    return _experts(h2, gd, u_tab, v_tab, y, tile=expert_tile, eblk=EXPERT_BLOCK)


def kernel(x_prompt, x_sample, cache_k, cache_v, cache_logf, state_conv, page_table, norm1_g, w_in, b_forget,
           q_gain, k_gain, conv_w, conv_b, conv_ln_g, conv_ln_b, w_out, norm2_g, peer_wq, peer_subkeys,
           peer_u, peer_v):
    assert w_in.shape[0] == 1, "single layer"
    batch, seq_len, _ = x_prompt.shape
    db = x_sample.shape[0]
    n_pool, page = cache_k.shape[1], cache_k.shape[2]

    w = w_in[0].astype(BF16)
    w_qkv = w[:, :3 * ATTN_W]
    w_ag = w[:, 3 * ATTN_W:3 * ATTN_W + 2 * CONV_CH]
    w_fg = jnp.pad(w[:, 3 * ATTN_W + 2 * CONV_CH:], ((0, 0), (0, LANES - N_HEADS)))
    b_fg = jnp.pad(b_forget[0].reshape(1, N_HEADS), ((0, 0), (0, LANES - N_HEADS)))
    qg = jnp.tile(q_gain[0], N_HEADS).reshape(1, ATTN_W)
    kg = jnp.tile(k_gain[0], N_HEADS).reshape(1, ATTN_W)
    gi = lax.broadcasted_iota(jnp.int32, (ATTN_W, ATTN_W), 0) // HEAD_DIM
    gj = lax.broadcasted_iota(jnp.int32, (ATTN_W, ATTN_W), 1) // HEAD_DIM
    bd = (gi == gj).astype(BF16)
    tri = (lax.broadcasted_iota(jnp.int32, (LANES, LANES), 1) <=
           lax.broadcasted_iota(jnp.int32, (LANES, LANES), 0)).astype(BF16)
    src = lax.broadcasted_iota(jnp.int32, (ATTN_W, N_HEADS * LANES), 0)
    dst = lax.broadcasted_iota(jnp.int32, (ATTN_W, N_HEADS * LANES), 1)
    place_k = (dst == (src // HEAD_DIM) * LANES + src % HEAD_DIM).astype(BF16)
    src = lax.broadcasted_iota(jnp.int32, (N_BIAS_TERMS * LANES, N_HEADS * LANES), 0)
    dst = lax.broadcasted_iota(jnp.int32, (N_BIAS_TERMS * LANES, N_HEADS * LANES), 1)
    place_c = ((src % LANES < N_HEADS) & (dst == (src % LANES) * LANES + HEAD_DIM + src // LANES)).astype(BF16)
    proj_wts = (norm1_g[0].reshape(1, D_MODEL), w_qkv, w_ag, w_fg, b_fg, qg, kg, bd, tri, place_k, place_c)
    conv_wts = (conv_w[0], conv_b[0].reshape(1, CONV_CH), conv_ln_g[0].reshape(1, CONV_CH),
                conv_ln_b[0].reshape(1, CONV_CH))
    wo = w_out[0].astype(BF16)
    sk = peer_subkeys[0].astype(BF16).reshape(2 * PEER_HEADS, PEER_KEYS, PEER_KEYS)
    route_wts = (wo[:ATTN_W], wo[ATTN_W:], norm2_g[0].reshape(1, D_MODEL), peer_wq[0].astype(BF16), sk)
    u_tab = peer_u[0].astype(BF16)
    v_tab = peer_v[0].astype(BF16)

    xp2 = x_prompt.reshape(batch * seq_len, D_MODEL)
    k_p, v_p, u_p, lf_p, qb, _, _, kaug, vt = _proj(
        xp2, proj_wts, tile=PROJ_TILE, seq_len=seq_len, with_cumsum=True)
    a_p = _attn(qb, kaug, vt, batch=batch, seq_len=seq_len, tq=ATTN_TILE, tk=ATTN_KEY_TILE)
    c_p = _conv_prompt(u_p, *conv_wts, tile=CONV_TILE, seq_len=seq_len)
    y_p = _finish(xp2, a_p, c_p, route_wts, u_tab, v_tab, expert_tile=EXPERT_TILE)

    xs2 = x_sample.reshape(db, D_MODEL)
    k_s, v_s, u_s, lf_s, qsb, ksb, vsb, _, _ = _proj(xs2, proj_wts, tile=db, seq_len=db, with_cumsum=False)
    lf_t = cache_logf[0].transpose(0, 2, 1)
    row = lambda a: a.reshape(db, 1, ATTN_W)
    transposed = lambda c: c[0].transpose(0, 2, 3, 1).reshape(n_pool, ATTN_W, page)
    a_s = _decode(page_table, row(qsb), row(ksb), row(vsb), lf_s.reshape(db, N_HEADS, 1),
                  transposed(cache_k), transposed(cache_v), lf_t,
                  pages_per_step=DECODE_PAGES_PER_STEP).reshape(db, ATTN_W)
    window = jnp.concatenate([state_conv[0], u_s[:, None, :]], axis=1)
    c_s = _conv_step(window.transpose(1, 0, 2), *conv_wts)
    y_s = _finish(xs2, a_s, c_s, route_wts, u_tab, v_tab, expert_tile=db)

    tail = CONV_W - 1
    return (y_p.reshape(batch, seq_len, D_MODEL),
            y_s.reshape(db, 1, D_MODEL),
            k_p.reshape(1, batch, seq_len, N_HEADS, HEAD_DIM),
            v_p.reshape(1, batch, seq_len, N_HEADS, HEAD_DIM),
            lf_p.reshape(1, batch, seq_len, N_HEADS),
            u_p.reshape(batch, seq_len, CONV_CH)[:, seq_len - tail:][None],
            k_s.reshape(1, db, 1, N_HEADS, HEAD_DIM),
            v_s.reshape(1, db, 1, N_HEADS, HEAD_DIM),
            lf_s.reshape(1, db, 1, N_HEADS),
            window[:, 1:][None])
```

```python
import functools
import math

import jax
import jax.numpy as jnp
from jax import lax
from jax.experimental import pallas as pl
from jax.experimental.pallas import tpu as pltpu

F32 = jnp.float32
BF16 = jnp.bfloat16

D_MODEL = 1024
HEAD_DIM = 64
N_HEADS = 8
ATTN_W = N_HEADS * HEAD_DIM
CONV_CH = D_MODEL - ATTN_W
CONV_W = 31
PEER_HEADS = 8
PEER_KEYS = 128
PEER_TOPK = 16
N_EXPERTS = PEER_KEYS * PEER_KEYS
EPS = 1e-6
LANES = 128
SUBLANES = 8
MXU_TILE = 256
VMEM_LIMIT = 56 * 1024 * 1024
NEG_INF = float("-inf")


def _cparams(sem):
    return pltpu.CompilerParams(dimension_semantics=sem, vmem_limit_bytes=VMEM_LIMIT)


def _split3(x):
    p1 = x.astype(BF16)
    r1 = x - p1.astype(F32)
    p2 = r1.astype(BF16)
    p3 = (r1 - p2.astype(F32)).astype(BF16)
    return p1, p2, p3


def _full_spec(shape):
    return pl.BlockSpec(shape, lambda *_: (0,) * len(shape))


def _group_rms(t, bd, gain):
    sq = t * t
    hi = sq.astype(BF16)
    lo = (sq - hi.astype(F32)).astype(BF16)
    ss = jnp.dot(hi, bd, preferred_element_type=F32) + jnp.dot(lo, bd, preferred_element_type=F32)
    return t * lax.rsqrt(ss * (1.0 / HEAD_DIM) + EPS) * gain


def _proj_kernel(x_ref, g1_ref, wqkv_ref, wag_ref, wfg_ref, bf_ref, qg_ref, kg_ref, bd_ref, tri_ref, pk_ref, pc_ref,
                 k_ref, v_ref, u_ref, lf_ref, qb_ref, kb_ref, vb_ref, kaug_ref, vt_ref,
                 carry_ref, *, tiles_per_seq, with_cumsum):
    t = x_ref.shape[0]
    x = x_ref[...]
    ms = jnp.mean(x * x, axis=-1, keepdims=True)
    h = (x * lax.rsqrt(ms + EPS) * g1_ref[...]).astype(BF16)

    qkv = jnp.dot(h, wqkv_ref[...], preferred_element_type=F32)
    bd = bd_ref[...]
    q = _group_rms(qkv[:, :ATTN_W], bd, qg_ref[...])
    k = _group_rms(qkv[:, ATTN_W:2 * ATTN_W], bd, kg_ref[...])
    v = qkv[:, 2 * ATTN_W:]
    v_t = v.T
    k_ref[...] = k.T
    v_ref[...] = v_t
    qb_ref[...] = (q * (HEAD_DIM ** -0.5)).astype(BF16)
    kb_ref[...] = k.astype(BF16)
    vb_ref[...] = v.astype(BF16)

    ag = jnp.dot(h, wag_ref[...], preferred_element_type=F32)
    u_ref[...] = ag[:, :CONV_CH] * jax.nn.sigmoid(ag[:, CONV_CH:])

    z = jnp.dot(h, wfg_ref[...], preferred_element_type=F32) + bf_ref[...]
    logf = jnp.minimum(z, 0.0) - jnp.log1p(jnp.exp(-jnp.abs(z)))
    lf_ref[...] = logf[:, :N_HEADS]

    if with_cumsum:
        vt_ref[...] = v_t.astype(BF16)
        i = pl.program_id(0)
        lane = lax.broadcasted_iota(jnp.int32, (1, LANES), 1)
        carry = jnp.where(i % tiles_per_seq == 0, 0.0, carry_ref[...])
        tri = tri_ref[...]
        for sb in range(t // LANES):
            rows = slice(sb * LANES, (sb + 1) * LANES)
            blk = jnp.where(lane < N_HEADS, logf[rows], 0.0)
            p1, p2, p3 = _split3(blk)
            cs = (jnp.dot(tri, p1, preferred_element_type=F32)
                  + jnp.dot(tri, p2, preferred_element_type=F32)
                  + jnp.dot(tri, p3, preferred_element_type=F32)) + carry
            carry = cs[LANES - 1:LANES, :]
            placed = (jnp.dot(k[rows].astype(BF16), pk_ref[...], preferred_element_type=F32)
                      + jnp.dot(jnp.concatenate(_split3(-cs), axis=1), pc_ref[...], preferred_element_type=F32))
            kaug_ref[rows, :] = placed.astype(BF16)
        carry_ref[...] = carry
    else:
        kaug_ref[...] = jnp.zeros_like(kaug_ref)
        vt_ref[...] = jnp.zeros_like(vt_ref)


def _proj(x2d, wts, *, tile, seq_len, with_cumsum):
    n = x2d.shape[0]
    nt = n // tile
    row = lambda w: pl.BlockSpec((tile, w), lambda i: (i, 0))
    tps = seq_len // tile
    kern = functools.partial(_proj_kernel, tiles_per_seq=tps, with_cumsum=with_cumsum)
    state_t = pl.BlockSpec((ATTN_W, tile), lambda i: (i // tps, i % tps))
    out_shape = (
        jax.ShapeDtypeStruct((n // seq_len * ATTN_W, seq_len), F32),
        jax.ShapeDtypeStruct((n // seq_len * ATTN_W, seq_len), F32),
        jax.ShapeDtypeStruct((n, CONV_CH), F32),
        jax.ShapeDtypeStruct((n, N_HEADS), F32),
        jax.ShapeDtypeStruct((n, ATTN_W), BF16),
        jax.ShapeDtypeStruct((n, ATTN_W), BF16),
        jax.ShapeDtypeStruct((n, ATTN_W), BF16),
        jax.ShapeDtypeStruct((n, N_HEADS * LANES), BF16),
        jax.ShapeDtypeStruct((ATTN_W, n), BF16),
    )
    out_specs = (state_t, state_t, row(CONV_CH), row(N_HEADS), row(ATTN_W), row(ATTN_W), row(ATTN_W),
                 row(N_HEADS * LANES), pl.BlockSpec((ATTN_W, tile), lambda i: (0, i)))
    in_specs = [row(D_MODEL)] + [_full_spec(w.shape) for w in wts]
    return pl.pallas_call(
        kern, grid=(nt,), in_specs=in_specs, out_specs=out_specs, out_shape=out_shape,
        scratch_shapes=[pltpu.VMEM((1, LANES), F32)],
        compiler_params=_cparams(("arbitrary",)), name="proj",
    )(x2d, *wts)


N_BIAS_TERMS = 3


def _attn_kernel(q_ref, kaug_ref, vt_ref, o_ref, acc_ref, sa_ref, sb_ref, *, tq, tk):
    qi = pl.program_id(2)
    lane = lax.broadcasted_iota(jnp.int32, (1, LANES), 1)
    q = q_ref[...].astype(F32)
    ones = jnp.where(lane < HEAD_DIM + N_BIAS_TERMS, 1.0, 0.0)
    key_row = lax.broadcasted_iota(jnp.int32, (tk, tq), 0)
    qry_col = lax.broadcasted_iota(jnp.int32, (tk, tq), 1)
    last = (qi * tq) // tk
    visible = key_row + last * tk <= qry_col + qi * tq
    qhs = []
    for hh in range(2):
        qh = q if hh == 0 else pltpu.roll(q, HEAD_DIM, axis=1)
        qhs.append(jnp.where(lane < HEAD_DIM, qh, ones).astype(BF16))
    acc_ref[...] = jnp.zeros_like(acc_ref)

    def scores(j, s_ref):
        start = pl.multiple_of(j * tk, tk)
        for hh in range(2):
            s_ref[hh] = lax.dot_general(kaug_ref[pl.ds(start, tk), hh * LANES:(hh + 1) * LANES], qhs[hh],
                                        (((1,), (1,)), ((), ())), preferred_element_type=F32)

    def consume(j, s_ref, stats, masked):
        start = pl.multiple_of(j * tk, tk)
        vt = vt_ref[:, pl.ds(start, tk)]
        new_stats = []
        for hh in range(2):
            m, l = stats[hh]
            s = jnp.where(visible, s_ref[hh], NEG_INF) if masked else s_ref[hh]
            m_new = jnp.maximum(m, jnp.max(s, axis=0, keepdims=True))
            alpha = jnp.exp(m - m_new)
            pr = jnp.exp(s - m_new)
            l = alpha * l + jnp.sum(pr, axis=0, keepdims=True)
            acc_ref[hh] = alpha * acc_ref[hh] + jnp.dot(vt, pr.astype(BF16), preferred_element_type=F32)
            new_stats.append((m_new, l))
        return tuple(new_stats)

    def pair(jj, stats):
        j = 2 * jj
        scores(j + 1, sb_ref)
        stats = consume(j, sa_ref, stats, False)
        scores(j + 2, sa_ref)
        return consume(j + 1, sb_ref, stats, False)

    def even_tail(stats):
        return consume(last, sa_ref, stats, True)

    def odd_tail(stats):
        scores(last, sb_ref)
        stats = consume(last - 1, sa_ref, stats, False)
        return consume(last, sb_ref, stats, True)

    init = (jnp.full((1, tq), NEG_INF, F32), jnp.zeros((1, tq), F32))
    scores(0, sa_ref)
    stats = lax.fori_loop(0, last // 2, pair, (init, init))
    stats = lax.cond(last % 2 == 0, even_tail, odd_tail, stats)
    row = lax.broadcasted_iota(jnp.int32, (LANES, 1), 0)
    out = jnp.where(row < HEAD_DIM, acc_ref[0] / stats[0][1], acc_ref[1] / stats[1][1])
    o_ref[...] = out.T.astype(o_ref.dtype)


def _attn(qb, kaug, vt, *, batch, seq_len, tq, tk):
    n = qb.shape[0]
    nq = seq_len // tq
    kern = functools.partial(_attn_kernel, tq=tq, tk=tk)
    return pl.pallas_call(
        kern, grid=(batch, N_HEADS // 2, nq),
        in_specs=[
            pl.BlockSpec((tq, LANES), lambda b, p, i: (b * nq + i, p)),
            pl.BlockSpec((seq_len, 2 * LANES), lambda b, p, i: (b, p)),
            pl.BlockSpec((LANES, seq_len), lambda b, p, i: (p, b)),
        ],
        out_specs=pl.BlockSpec((tq, LANES), lambda b, p, i: (b * nq + i, p)),
        out_shape=jax.ShapeDtypeStruct((n, ATTN_W), BF16),
        scratch_shapes=[pltpu.VMEM((2, LANES, tq), F32), pltpu.VMEM((2, tk, tq), F32), pltpu.VMEM((2, tk, tq), F32)],
        compiler_params=_cparams(("arbitrary", "arbitrary", "arbitrary")), name="attn",
    )(qb, kaug, vt)


def _decode_kernel(pt_ref, q_ref, kn_ref, vn_ref, lfn_ref, su_ref, *rest, pages_per_step):
    del pt_ref
    pp = pages_per_step
    k_refs = rest[:pp]
    v_refs = rest[pp:2 * pp]
    lf_refs = rest[2 * pp:3 * pp]
    o_ref, m_ref, l_ref, acc_ref, carry_ref = rest[3 * pp:]
    j = pl.program_id(1)
    nj = pl.num_programs(1)
    page = lf_refs[0].shape[1]

    sub = lax.broadcasted_iota(jnp.int32, (N_HEADS, ATTN_W), 0)
    lane = lax.broadcasted_iota(jnp.int32, (N_HEADS, ATTN_W), 1)
    headmask = (lane // HEAD_DIM) == sub
    qbd = jnp.where(headmask, jnp.broadcast_to(q_ref[...].astype(F32), (N_HEADS, ATTN_W)), 0.0)
    qbd = qbd.astype(BF16)

    @pl.when(j == 0)
    def _():
        m_ref[...] = jnp.full_like(m_ref, NEG_INF)
        l_ref[...] = jnp.zeros_like(l_ref)
        acc_ref[...] = jnp.zeros_like(acc_ref)
        carry_ref[...] = jnp.broadcast_to(lfn_ref[...], carry_ref.shape)

    parts = []
    for r in lf_refs:
        parts.extend(_split3(r[...]))
    sums = jnp.dot(jnp.concatenate(parts, axis=0), su_ref[...], preferred_element_type=F32)
    carry = carry_ref[...]
    ss = []
    for i in range(pp):
        base = 3 * N_HEADS * i
        both = sums[base:base + 8] + sums[base + 8:base + 16] + sums[base + 16:base + 24]
        s = jnp.dot(qbd, k_refs[i][...].astype(BF16), preferred_element_type=F32)
        ss.append(s + both[:, :page] + carry)
        carry = carry + both[:, page:]
    carry_ref[...] = carry
    m_old = m_ref[...]
    m = m_old
    for s in ss:
        m = jnp.maximum(m, s)
    alpha = jnp.exp(m_old - m)
    prs = [jnp.exp(s - m) for s in ss]
    l = alpha * l_ref[...]
    for pr in prs:
        l = l + pr
    for h in range(N_HEADS):
        rows = slice(h * HEAD_DIM, (h + 1) * HEAD_DIM)
        acc = acc_ref[rows, :] * alpha[h:h + 1, :]
        for i in range(pp):
            acc = acc + v_refs[i][rows, :] * prs[i][h:h + 1, :]
        acc_ref[rows, :] = acc
    m_ref[...] = m
    l_ref[...] = l

    @pl.when(j == nj - 1)
    def _():
        s_self = jnp.sum(qbd.astype(F32) * kn_ref[...].astype(F32), axis=-1, keepdims=True)
        m_all = jnp.maximum(jnp.max(m, axis=-1, keepdims=True), s_self)
        w = jnp.exp(m - m_all)
        w_self = jnp.exp(s_self - m_all)
        denom = jnp.sum(l * w, axis=-1, keepdims=True) + w_self
        ones = jnp.ones((N_HEADS, page), BF16)
        o = jnp.zeros((N_HEADS, ATTN_W), F32)
        weighted = jnp.concatenate(
            [acc_ref[h * HEAD_DIM:(h + 1) * HEAD_DIM, :] * w[h:h + 1, :] for h in range(N_HEADS)], axis=0)
        for part in _split3(weighted):
            o = o + lax.dot_general(ones, part, (((1,), (1,)), ((), ())), preferred_element_type=F32)
        o = o + w_self * jnp.broadcast_to(vn_ref[...].astype(F32), (N_HEADS, ATTN_W))
        o = jnp.where(headmask, o / denom, 0.0)
        o_ref[...] = jnp.sum(o, axis=0, keepdims=True).astype(o_ref.dtype)


def _decode(page_table, qb, kb, vb, lf_new, cache_kt, cache_vt, cache_lf_t, *, pages_per_step):
    db, n_pages = page_table.shape
    pp = pages_per_step
    nj = n_pages // pp
    page = cache_lf_t.shape[2]
    newer = (lax.broadcasted_iota(jnp.int32, (page, page), 0) >
             lax.broadcasted_iota(jnp.int32, (page, page), 1))
    su = jnp.concatenate([newer.astype(BF16), jnp.ones((page, page), BF16)], axis=1)

    def page_idx(i):
        return lambda b, j, pt: (pt[b, n_pages - 1 - (j * pp + i)], 0, 0)

    per_seq = lambda: pl.BlockSpec((None, 1, ATTN_W), lambda b, j, pt: (b, 0, 0))
    in_specs = [per_seq(), per_seq(), per_seq(),
                pl.BlockSpec((None, N_HEADS, 1), lambda b, j, pt: (b, 0, 0)),
                pl.BlockSpec((page, 2 * page), lambda b, j, pt: (0, 0))]
    in_specs += [pl.BlockSpec((None, ATTN_W, page), page_idx(i)) for i in range(pp)]
    in_specs += [pl.BlockSpec((None, ATTN_W, page), page_idx(i)) for i in range(pp)]
    in_specs += [pl.BlockSpec((None, N_HEADS, page), page_idx(i)) for i in range(pp)]
    grid_spec = pltpu.PrefetchScalarGridSpec(
        num_scalar_prefetch=1, grid=(db, nj), in_specs=in_specs,
        out_specs=per_seq(),
        scratch_shapes=[pltpu.VMEM((N_HEADS, page), F32), pltpu.VMEM((N_HEADS, page), F32),
                        pltpu.VMEM((ATTN_W, page), F32), pltpu.VMEM((N_HEADS, page), F32)])
    kern = functools.partial(_decode_kernel, pages_per_step=pp)
    return pl.pallas_call(
        kern, grid_spec=grid_spec, out_shape=jax.ShapeDtypeStruct((db, 1, ATTN_W), BF16),
        compiler_params=_cparams(("arbitrary", "arbitrary")), name="decode",
    )(page_table, qb, kb, vb, lf_new, su, *([cache_kt] * pp), *([cache_vt] * pp), *([cache_lf_t] * pp))


CONV_HALO = 32
CONV_ROWS = 32


def _ln_silu(y, g, b):
    mu = jnp.mean(y, axis=-1, keepdims=True)
    d = y - mu
    var = jnp.mean(d * d, axis=-1, keepdims=True)
    z = d * lax.rsqrt(var + EPS) * g + b
    return z * jax.nn.sigmoid(z)


def _conv_kernel(cur_ref, prev_ref, w_ref, b_ref, g_ref, lb_ref, o_ref, xs_ref, *, tiles_per_seq):
    t = cur_ref.shape[0]
    i = pl.program_id(0)
    xs_ref[0, 0:CONV_HALO, :] = jnp.where(i % tiles_per_seq == 0, 0.0, prev_ref[...])
    xs_ref[0, CONV_HALO:, :] = cur_ref[...]
    off = CONV_HALO - (CONV_W - 1)
    span = t + CONV_HALO - SUBLANES
    for r in range(1, SUBLANES):
        xs_ref[r, 0:span, :] = xs_ref[0, pl.ds(r, span), :]
    for c in range(t // CONV_ROWS):
        acc = jnp.broadcast_to(b_ref[...], (CONV_ROWS, CONV_CH))
        for w in range(CONV_W):
            r, base = (off + w) % SUBLANES, (off + w) // SUBLANES * SUBLANES
            acc = acc + xs_ref[r, pl.ds(c * CONV_ROWS + base, CONV_ROWS), :] * w_ref[w:w + 1, :]
        o_ref[c * CONV_ROWS:(c + 1) * CONV_ROWS, :] = _ln_silu(acc, g_ref[...], lb_ref[...]).astype(o_ref.dtype)


def _conv_prompt(u2d, conv_w, conv_b, ln_g, ln_b, *, tile, seq_len):
    n = u2d.shape[0]
    ratio = tile // CONV_HALO
    kern = functools.partial(_conv_kernel, tiles_per_seq=seq_len // tile)
    wts = (conv_w, conv_b, ln_g, ln_b)
    return pl.pallas_call(
        kern, grid=(n // tile,),
        in_specs=[pl.BlockSpec((tile, CONV_CH), lambda i: (i, 0)),
                  pl.BlockSpec((CONV_HALO, CONV_CH), lambda i: (jnp.maximum(i * ratio - 1, 0), 0))]
                 + [_full_spec(w.shape) for w in wts],
        out_specs=pl.BlockSpec((tile, CONV_CH), lambda i: (i, 0)),
        out_shape=jax.ShapeDtypeStruct((n, CONV_CH), BF16),
        scratch_shapes=[pltpu.VMEM((SUBLANES, tile + CONV_HALO, CONV_CH), F32)],
        compiler_params=_cparams(("arbitrary",)), name="conv_prompt",
    )(u2d, u2d, *wts)


def _conv_step_kernel(xp_ref, w_ref, b_ref, g_ref, lb_ref, o_ref):
    acc = jnp.broadcast_to(b_ref[...], o_ref.shape)
    for w in range(CONV_W):
        acc = acc + xp_ref[w] * w_ref[w:w + 1, :]
    o_ref[...] = _ln_silu(acc, g_ref[...], lb_ref[...]).astype(o_ref.dtype)


def _conv_step(xp_t, conv_w, conv_b, ln_g, ln_b):
    db = xp_t.shape[1]
    args = (xp_t, conv_w, conv_b, ln_g, ln_b)
    return pl.pallas_call(
        _conv_step_kernel, grid=(1,),
        in_specs=[_full_spec(a.shape) for a in args],
        out_specs=_full_spec((db, CONV_CH)),
        out_shape=jax.ShapeDtypeStruct((db, CONV_CH), BF16),
        compiler_params=_cparams(("arbitrary",)), name="conv_step",
    )(*args)


N_CAND = 64
ROUTE_CHAINS = 2


def _top16(chains, iota_f):
    ss = [c[0] for c in chains]
    for r in range(PEER_TOPK):
        for n, (_, val_ref, idx_ref) in enumerate(chains):
            m = jnp.max(ss[n], axis=0, keepdims=True)
            idx = jnp.min(jnp.where(ss[n] == m, iota_f, float(1 << 20)), axis=0, keepdims=True)
            val_ref[r:r + 1, :] = m
            idx_ref[r:r + 1, :] = idx
            ss[n] = jnp.where(iota_f == idx, NEG_INF, ss[n])


def _batcher_network(n):
    pairs = []
    p = 1
    while p < n:
        k = p
        while k >= 1:
            for j in range(k % p, n - k, 2 * k):
                for i in range(min(k, n - j - k)):
                    if (i + j) // (2 * p) == (i + j + k) // (2 * p):
                        pairs.append((i + j, i + j + k))
            k //= 2
        p *= 2
    return pairs


def _top16_columns(srcs, outs):
    t = srcs[0].shape[1]
    groups = PEER_KEYS // SUBLANES
    sub = lax.broadcasted_iota(jnp.int32, (SUBLANES, t), 0).astype(F32)
    vals = [[s[SUBLANES * j:SUBLANES * (j + 1), :] for j in range(groups)] for s in srcs]
    idxs = [[sub + float(SUBLANES * j) for j in range(groups)] for _ in srcs]
    for a, b in _batcher_network(groups):
        for v, ix in zip(vals, idxs):
            swap = v[b] > v[a]
            v[a], v[b] = jnp.maximum(v[a], v[b]), jnp.minimum(v[a], v[b])
            ix[a], ix[b] = jnp.where(swap, ix[b], ix[a]), jnp.where(swap, ix[a], ix[b])
    flag = jnp.zeros((SUBLANES, t), F32)
    for v in vals:
        for d in range(groups - 1):
            flag = jnp.maximum(flag, jnp.where(v[d] == v[d + 1], 1.0, 0.0))
    for r in range(PEER_TOPK):
        for v, ix, (val_ref, idx_ref) in zip(vals, idxs, outs):
            m = jnp.max(v[0], axis=0, keepdims=True)
            idx = jnp.min(jnp.where(v[0] == m, ix[0], float(1 << 20)), axis=0, keepdims=True)
            val_ref[r:r + 1, :] = m
            idx_ref[r:r + 1, :] = idx
            won = ix[0] == idx
            for d in range(PEER_TOPK - 1 - r):
                v[d] = jnp.where(won, v[d + 1], v[d])
                ix[d] = jnp.where(won, ix[d + 1], ix[d])
    return flag


def _candidate_ids(t):
    row = lax.broadcasted_iota(jnp.int32, (N_CAND, t), 0)
    r1 = jnp.where(row < 16, 0, jnp.where(row < 40, (row - 8) // 8, jnp.where(row < 56, (row - 24) // 4, row - 48)))
    r2 = jnp.where(row < 16, row, jnp.where(row < 40, row % 8, jnp.where(row < 56, row % 4, 0)))
    return (r1 * PEER_TOPK + r2).astype(F32)


def _route_kernel(x_ref, a_ref, c_ref, woa_ref, woc_ref, g2_ref, wq_ref, sk_ref,
                  y_ref, h_ref, i1_ref, i2_ref, g_ref,
                  sc_ref, tv_ref, ti_ref, cand_ref, bs_ref, o1_ref, o2_ref, og_ref):
    t = x_ref.shape[0]
    y = (x_ref[...] + jnp.dot(a_ref[...], woa_ref[...], preferred_element_type=F32)
         + jnp.dot(c_ref[...], woc_ref[...], preferred_element_type=F32))
    y_ref[...] = y
    ms = jnp.mean(y * y, axis=-1, keepdims=True)
    h = (y * lax.rsqrt(ms + EPS) * g2_ref[...]).astype(BF16)
    h_ref[...] = h
    q = jnp.dot(h, wq_ref[...], preferred_element_type=F32).astype(BF16)
    for hp in range(2 * PEER_HEADS):
        sc_ref[hp] = lax.dot_general(sk_ref[hp], q[:, hp * PEER_KEYS:(hp + 1) * PEER_KEYS],
                                     (((1,), (1,)), ((), ())), preferred_element_type=F32)

    key_iota = lax.broadcasted_iota(jnp.int32, (PEER_KEYS, t), 0).astype(F32)

    def stage1(it, c):
        hps = [it * ROUTE_CHAINS + n for n in range(ROUTE_CHAINS)]
        outs = [(tv_ref.at[hp], ti_ref.at[hp]) for hp in hps]
        unordered_ties = _top16_columns([sc_ref[hp] for hp in hps], outs)

        @pl.when(jnp.max(unordered_ties) > 0.0)
        def _():
            _top16([(sc_ref[hp], val_ref, idx_ref) for hp, (val_ref, idx_ref) in zip(hps, outs)], key_iota)

        return c

    lax.fori_loop(0, 2 * PEER_HEADS // ROUTE_CHAINS, stage1, 0)

    cand_id = _candidate_ids(t)
    rank_iota = lax.broadcasted_iota(jnp.int32, (PEER_TOPK, t), 0).astype(F32)
    low4 = lax.broadcasted_iota(jnp.int32, (SUBLANES, t), 0) < 4

    def stage2(it, c):
        heads = [it * ROUTE_CHAINS + n for n in range(ROUTE_CHAINS)]
        ss, i1s, i2s = [], [], []
        for n, hd in enumerate(heads):
            v1 = tv_ref[2 * hd]
            v2 = tv_ref[2 * hd + 1]
            i1s.append(ti_ref[2 * hd])
            i2s.append(ti_ref[2 * hd + 1])
            v2lo = v2[0:SUBLANES, :]
            v2q = jnp.where(low4, v2lo, pltpu.roll(v2lo, 4, axis=0))
            cand_ref[n, 0:16, :] = v1[0:1, :] + v2
            for r1 in (1, 2, 3):
                cand_ref[n, 8 + 8 * r1:16 + 8 * r1, :] = v1[r1:r1 + 1, :] + v2lo
            cand_ref[n, 40:48, :] = jnp.where(low4, v1[4:5, :], v1[5:6, :]) + v2q
            cand_ref[n, 48:56, :] = jnp.where(low4, v1[6:7, :], v1[7:8, :]) + v2q
            cand_ref[n, 56:64, :] = v1[8:16, :] + v2[0:1, :]
            ss.append(cand_ref[n])
        for r in range(PEER_TOPK):
            for n, hd in enumerate(heads):
                base = pl.multiple_of(hd * PEER_TOPK, PEER_TOPK)
                m = jnp.max(ss[n], axis=0, keepdims=True)
                cid = jnp.min(jnp.where(ss[n] == m, cand_id, float(1 << 20)), axis=0, keepdims=True)
                ss[n] = jnp.where(cand_id == cid, NEG_INF, ss[n])
                r1 = jnp.floor(cid * (1.0 / PEER_TOPK))
                r2 = cid - r1 * PEER_TOPK
                bs_ref[n, r:r + 1, :] = m
                o1_ref[pl.ds(base + r, 1), :] = jnp.max(jnp.where(rank_iota == r1, i1s[n], -1.0), axis=0, keepdims=True)
                o2_ref[pl.ds(base + r, 1), :] = jnp.max(jnp.where(rank_iota == r2, i2s[n], -1.0), axis=0, keepdims=True)
        for n, hd in enumerate(heads):
            base = pl.multiple_of(hd * PEER_TOPK, PEER_TOPK)
            bs = bs_ref[n]
            e = jnp.exp(bs - bs[0:1, :])
            og_ref[pl.ds(base, PEER_TOPK), :] = e / jnp.sum(e, axis=0, keepdims=True)
        return c

    lax.fori_loop(0, PEER_HEADS // ROUTE_CHAINS, stage2, 0)
    i1_ref[...] = o1_ref[...].T
    i2_ref[...] = o2_ref[...].T
    g_ref[...] = og_ref[...].T


def _route(x2d, a2d, c2d, wts, *, tile):
    n = x2d.shape[0]
    row = lambda w: pl.BlockSpec((tile, w), lambda i: (i, 0))
    hk = PEER_HEADS * PEER_TOPK
    out_shape = (jax.ShapeDtypeStruct((n, D_MODEL), F32),
                 jax.ShapeDtypeStruct((n, D_MODEL), BF16),
                 jax.ShapeDtypeStruct((n, hk), F32),
                 jax.ShapeDtypeStruct((n, hk), F32),
                 jax.ShapeDtypeStruct((n, hk), F32))
    return pl.pallas_call(
        _route_kernel, grid=(n // tile,),
        in_specs=[row(D_MODEL), row(ATTN_W), row(CONV_CH)] + [_full_spec(w.shape) for w in wts],
        out_specs=(row(D_MODEL), row(D_MODEL), row(hk), row(hk), row(hk)),
        out_shape=out_shape,
        scratch_shapes=[pltpu.VMEM((2 * PEER_HEADS, PEER_KEYS, tile), F32),
                        pltpu.VMEM((2 * PEER_HEADS, PEER_TOPK, tile), F32),
                        pltpu.VMEM((2 * PEER_HEADS, PEER_TOPK, tile), F32),
                        pltpu.VMEM((ROUTE_CHAINS, N_CAND, tile), F32),
                        pltpu.VMEM((ROUTE_CHAINS, PEER_TOPK, tile), F32),
                        pltpu.VMEM((hk, tile), F32), pltpu.VMEM((hk, tile), F32), pltpu.VMEM((hk, tile), F32)],
        compiler_params=_cparams(("arbitrary",)), name="route",
    )(x2d, a2d, c2d, *wts)


GATE_PITCH = PEER_KEYS + SUBLANES
GATE_UNROLL = 16


def _gates_kernel(i1_ref, i2_ref, g_ref, o_ref, stage_ref):
    t = i1_ref.shape[0]
    hk = i1_ref.shape[1]
    sub = lax.broadcasted_iota(jnp.int32, (PEER_KEYS, hk), 0).astype(F32)

    def body(tb, c):
        for u in range(GATE_UNROLL):
            tok = tb * GATE_UNROLL + u
            i1 = i1_ref[pl.ds(tok, 1), :]
            i2 = i2_ref[pl.ds(tok, 1), :]
            g = g_ref[pl.ds(tok, 1), :]
            p1 = jnp.where(sub == i1, g, 0.0).astype(BF16)
            p2 = jnp.where(sub == i2, 1.0, 0.0).astype(BF16)
            grid = lax.dot_general(p1, p2, (((1,), (1,)), ((), ())), preferred_element_type=F32)
            stage_ref[pl.ds(pl.multiple_of(tok * GATE_PITCH, SUBLANES), PEER_KEYS), :] = grid
        return c

    lax.fori_loop(0, t // GATE_UNROLL, body, 0)

    def gather(a, c):
        rows = stage_ref[pl.ds(a, t, stride=GATE_PITCH), :]
        o_ref[:, pl.ds(pl.multiple_of(a * PEER_KEYS, PEER_KEYS), PEER_KEYS)] = rows.astype(o_ref.dtype)
        return c

    lax.fori_loop(0, PEER_KEYS, gather, 0, unroll=4)


def _gates(i1, i2, g, *, tile):
    n, hk = i1.shape
    row = pl.BlockSpec((tile, hk), lambda i: (i, 0))
    return pl.pallas_call(
        _gates_kernel, grid=(n // tile,), in_specs=[row, row, row],
        out_specs=pl.BlockSpec((tile, N_EXPERTS), lambda i: (i, 0)),
        out_shape=jax.ShapeDtypeStruct((n, N_EXPERTS), BF16),
        scratch_shapes=[pltpu.VMEM((tile * GATE_PITCH, PEER_KEYS), F32)],
        compiler_params=_cparams(("arbitrary",)), name="gates",
    )(i1, i2, g)


def _experts_kernel(h_ref, g_ref, u_ref, v_ref, y_ref, o_ref, acc_ref, even_ref, odd_ref):
    e = pl.program_id(1)

    @pl.when(e == 0)
    def _():
        acc_ref[...] = jnp.zeros_like(acc_ref)
        odd_ref[...] = jnp.zeros_like(odd_ref)

    def work(cur_ref, prev_ref):
        h = h_ref[...]
        for c in range(u_ref.shape[0] // MXU_TILE):
            cols = slice(c * MXU_TILE, (c + 1) * MXU_TILE)
            s = lax.dot_general(h, u_ref[cols, :], (((1,), (1,)), ((), ())), preferred_element_type=F32)
            act = 0.5 * s * (1.0 + lax.erf(s * math.sqrt(0.5)))
            cur_ref[:, cols] = (g_ref[:, cols].astype(F32) * act).astype(BF16)
        prev = prev_ref[...]
        for c in range(v_ref.shape[1] // MXU_TILE):
            cols = slice(c * MXU_TILE, (c + 1) * MXU_TILE)
            acc_ref[:, cols] += jnp.dot(prev, v_ref[:, cols], preferred_element_type=F32)

    pl.when(e % 2 == 0)(lambda: work(even_ref, odd_ref))
    pl.when(e % 2 == 1)(lambda: work(odd_ref, even_ref))

    @pl.when(e == pl.num_programs(1) - 1)
    def _():
        o_ref[...] = y_ref[...] + acc_ref[...]


def _experts(h2d, gdense, u_tab, v_tab, y2d, *, tile, eblk):
    n = h2d.shape[0]
    nb = N_EXPERTS // eblk
    scored = lambda e: jnp.minimum(e, nb - 1)
    folded = lambda e: jnp.maximum(e - 1, 0)
    return pl.pallas_call(
        _experts_kernel, grid=(n // tile, nb + 1),
        in_specs=[pl.BlockSpec((tile, D_MODEL), lambda i, e: (i, 0)),
                  pl.BlockSpec((tile, eblk), lambda i, e: (i, scored(e))),
                  pl.BlockSpec((eblk, D_MODEL), lambda i, e: (scored(e), 0)),
                  pl.BlockSpec((eblk, D_MODEL), lambda i, e: (folded(e), 0)),
                  pl.BlockSpec((tile, D_MODEL), lambda i, e: (i, 0))],
        out_specs=pl.BlockSpec((tile, D_MODEL), lambda i, e: (i, 0)),
        out_shape=jax.ShapeDtypeStruct((n, D_MODEL), F32),
        scratch_shapes=[pltpu.VMEM((tile, D_MODEL), F32), pltpu.VMEM((tile, eblk), BF16),
                        pltpu.VMEM((tile, eblk), BF16)],
        compiler_params=_cparams(("arbitrary", "arbitrary")), name="experts",
    )(h2d, gdense, u_tab, v_tab, y2d)


PROJ_TILE = 512
ATTN_TILE = 256
ATTN_KEY_TILE = 512
CONV_TILE = 256
ROUTE_TILE = 128
GATES_TILE = 64
EXPERT_TILE = 1024
EXPERT_BLOCK = 512
DECODE_PAGES_PER_STEP = 8


def _finish(x2d, a2d, c2d, route_wts, u_tab, v_tab, *, expert_tile):
    n = x2d.shape[0]
    y, h2, i1, i2, g = _route(x2d, a2d, c2d, route_wts, tile=min(ROUTE_TILE, n))
    gd = _gates(i1, i2, g, tile=GATES_TILE)
    return _experts(h2, gd, u_tab, v_tab, y, tile=expert_tile, eblk=EXPERT_BLOCK)


def kernel(x_prompt, x_sample, cache_k, cache_v, cache_logf, state_conv, page_table, norm1_g, w_in, b_forget,
           q_gain, k_gain, conv_w, conv_b, conv_ln_g, conv_ln_b, w_out, norm2_g, peer_wq, peer_subkeys,
           peer_u, peer_v):
    assert w_in.shape[0] == 1, "single layer"
    batch, seq_len, _ = x_prompt.shape
    db = x_sample.shape[0]
    n_pool, page = cache_k.shape[1], cache_k.shape[2]

    w = w_in[0].astype(BF16)
    w_qkv = w[:, :3 * ATTN_W]
    w_ag = w[:, 3 * ATTN_W:3 * ATTN_W + 2 * CONV_CH]
    w_fg = jnp.pad(w[:, 3 * ATTN_W + 2 * CONV_CH:], ((0, 0), (0, LANES - N_HEADS)))
    b_fg = jnp.pad(b_forget[0].reshape(1, N_HEADS), ((0, 0), (0, LANES - N_HEADS)))
    qg = jnp.tile(q_gain[0], N_HEADS).reshape(1, ATTN_W)
    kg = jnp.tile(k_gain[0], N_HEADS).reshape(1, ATTN_W)
    gi = lax.broadcasted_iota(jnp.int32, (ATTN_W, ATTN_W), 0) // HEAD_DIM
    gj = lax.broadcasted_iota(jnp.int32, (ATTN_W, ATTN_W), 1) // HEAD_DIM
    bd = (gi == gj).astype(BF16)
    tri = (lax.broadcasted_iota(jnp.int32, (LANES, LANES), 1) <=
           lax.broadcasted_iota(jnp.int32, (LANES, LANES), 0)).astype(BF16)
    src = lax.broadcasted_iota(jnp.int32, (ATTN_W, N_HEADS * LANES), 0)
    dst = lax.broadcasted_iota(jnp.int32, (ATTN_W, N_HEADS * LANES), 1)
    place_k = (dst == (src // HEAD_DIM) * LANES + src % HEAD_DIM).astype(BF16)
    src = lax.broadcasted_iota(jnp.int32, (N_BIAS_TERMS * LANES, N_HEADS * LANES), 0)
    dst = lax.broadcasted_iota(jnp.int32, (N_BIAS_TERMS * LANES, N_HEADS * LANES), 1)
    place_c = ((src % LANES < N_HEADS) & (dst == (src % LANES) * LANES + HEAD_DIM + src // LANES)).astype(BF16)
    proj_wts = (norm1_g[0].reshape(1, D_MODEL), w_qkv, w_ag, w_fg, b_fg, qg, kg, bd, tri, place_k, place_c)
    conv_wts = (conv_w[0], conv_b[0].reshape(1, CONV_CH), conv_ln_g[0].reshape(1, CONV_CH),
                conv_ln_b[0].reshape(1, CONV_CH))
    wo = w_out[0].astype(BF16)
    sk = peer_subkeys[0].astype(BF16).reshape(2 * PEER_HEADS, PEER_KEYS, PEER_KEYS)
    route_wts = (wo[:ATTN_W], wo[ATTN_W:], norm2_g[0].reshape(1, D_MODEL), peer_wq[0].astype(BF16), sk)
    u_tab = peer_u[0].astype(BF16)
    v_tab = peer_v[0].astype(BF16)

    xp2 = x_prompt.reshape(batch * seq_len, D_MODEL)
    k_p, v_p, u_p, lf_p, qb, _, _, kaug, vt = _proj(
        xp2, proj_wts, tile=PROJ_TILE, seq_len=seq_len, with_cumsum=True)
    a_p = _attn(qb, kaug, vt, batch=batch, seq_len=seq_len, tq=ATTN_TILE, tk=ATTN_KEY_TILE)
    c_p = _conv_prompt(u_p, *conv_wts, tile=CONV_TILE, seq_len=seq_len)
    y_p = _finish(xp2, a_p, c_p, route_wts, u_tab, v_tab, expert_tile=EXPERT_TILE)

    xs2 = x_sample.reshape(db, D_MODEL)
    k_s, v_s, u_s, lf_s, qsb, ksb, vsb, _, _ = _proj(xs2, proj_wts, tile=db, seq_len=db, with_cumsum=False)
    lf_t = cache_logf[0].transpose(0, 2, 1)
    row = lambda a: a.reshape(db, 1, ATTN_W)
    transposed = lambda c: c[0].transpose(0, 2, 3, 1).reshape(n_pool, ATTN_W, page)
    a_s = _decode(page_table, row(qsb), row(ksb), row(vsb), lf_s.reshape(db, N_HEADS, 1),
                  transposed(cache_k), transposed(cache_v), lf_t,
                  pages_per_step=DECODE_PAGES_PER_STEP).reshape(db, ATTN_W)
    window = jnp.concatenate([state_conv[0], u_s[:, None, :]], axis=1)
    c_s = _conv_step(window.transpose(1, 0, 2), *conv_wts)
    y_s = _finish(xs2, a_s, c_s, route_wts, u_tab, v_tab, expert_tile=db)

    tail = CONV_W - 1
    state = lambda t, seqs, toks: t.reshape(seqs, N_HEADS, HEAD_DIM, toks).transpose(0, 3, 1, 2)[None]
    return (y_p.reshape(batch, seq_len, D_MODEL),
            y_s.reshape(db, 1, D_MODEL),
            state(k_p, batch, seq_len),
            state(v_p, batch, seq_len),
            lf_p.reshape(1, batch, seq_len, N_HEADS),
            u_p.reshape(batch, seq_len, CONV_CH)[:, seq_len - tail:][None],
            state(k_s, 1, db).reshape(1, db, 1, N_HEADS, HEAD_DIM),
            state(v_s, 1, db).reshape(1, db, 1, N_HEADS, HEAD_DIM),
            lf_s.reshape(1, db, 1, N_HEADS),
            window[:, 1:][None])
```

```python
import functools
import math

import jax
import jax.numpy as jnp
from jax import lax
from jax.experimental import pallas as pl
from jax.experimental.pallas import tpu as pltpu

F32 = jnp.float32
BF16 = jnp.bfloat16

D_MODEL = 1024
HEAD_DIM = 64
N_HEADS = 8
ATTN_W = N_HEADS * HEAD_DIM
CONV_CH = D_MODEL - ATTN_W
CONV_W = 31
PEER_HEADS = 8
PEER_KEYS = 128
PEER_TOPK = 16
N_EXPERTS = PEER_KEYS * PEER_KEYS
EPS = 1e-6
LANES = 128
SUBLANES = 8
MXU_TILE = 256
VMEM_LIMIT = 56 * 1024 * 1024
NEG_INF = float("-inf")


def _cparams(sem):
    return pltpu.CompilerParams(dimension_semantics=sem, vmem_limit_bytes=VMEM_LIMIT)


def _split3(x):
    p1 = x.astype(BF16)
    r1 = x - p1.astype(F32)
    p2 = r1.astype(BF16)
    p3 = (r1 - p2.astype(F32)).astype(BF16)
    return p1, p2, p3


def _full_spec(shape):
    return pl.BlockSpec(shape, lambda *_: (0,) * len(shape))


def _group_rms(t, bd, gain):
    sq = t * t
    hi = sq.astype(BF16)
    lo = (sq - hi.astype(F32)).astype(BF16)
    ss = jnp.dot(hi, bd, preferred_element_type=F32) + jnp.dot(lo, bd, preferred_element_type=F32)
    return t * lax.rsqrt(ss * (1.0 / HEAD_DIM) + EPS) * gain


def _proj_kernel(x_ref, g1_ref, wqkv_ref, wag_ref, wfg_ref, bf_ref, qg_ref, kg_ref, bd_ref, tri_ref, pk_ref, pc_ref,
                 k_ref, v_ref, u_ref, lf_ref, qb_ref, kb_ref, vb_ref, kaug_ref, vt_ref,
                 carry_ref, *, tiles_per_seq, with_cumsum):
    t = x_ref.shape[0]
    x = x_ref[...]
    ms = jnp.mean(x * x, axis=-1, keepdims=True)
    h = (x * lax.rsqrt(ms + EPS) * g1_ref[...]).astype(BF16)

    qkv = jnp.dot(h, wqkv_ref[...], preferred_element_type=F32)
    bd = bd_ref[...]
    q = _group_rms(qkv[:, :ATTN_W], bd, qg_ref[...])
    k = _group_rms(qkv[:, ATTN_W:2 * ATTN_W], bd, kg_ref[...])
    v = qkv[:, 2 * ATTN_W:]
    v_t = v.T
    k_ref[...] = k.T
    v_ref[...] = v_t
    qb_ref[...] = (q * (HEAD_DIM ** -0.5)).astype(BF16)
    kb_ref[...] = k.astype(BF16)
    vb_ref[...] = v.astype(BF16)

    ag = jnp.dot(h, wag_ref[...], preferred_element_type=F32)
    u_ref[...] = ag[:, :CONV_CH] * jax.nn.sigmoid(ag[:, CONV_CH:])

    z = jnp.dot(h, wfg_ref[...], preferred_element_type=F32) + bf_ref[...]
    logf = jnp.minimum(z, 0.0) - jnp.log1p(jnp.exp(-jnp.abs(z)))
    lf_ref[...] = logf[:, :N_HEADS]

    if with_cumsum:
        vt_ref[...] = v_t.astype(BF16)
        i = pl.program_id(0)
        lane = lax.broadcasted_iota(jnp.int32, (1, LANES), 1)
        carry = jnp.where(i % tiles_per_seq == 0, 0.0, carry_ref[...])
        tri = tri_ref[...]
        for sb in range(t // LANES):
            rows = slice(sb * LANES, (sb + 1) * LANES)
            blk = jnp.where(lane < N_HEADS, logf[rows], 0.0)
            p1, p2, p3 = _split3(blk)
            cs = (jnp.dot(tri, p1, preferred_element_type=F32)
                  + jnp.dot(tri, p2, preferred_element_type=F32)
                  + jnp.dot(tri, p3, preferred_element_type=F32)) + carry
            carry = cs[LANES - 1:LANES, :]
            placed = (jnp.dot(k[rows].astype(BF16), pk_ref[...], preferred_element_type=F32)
                      + jnp.dot(jnp.concatenate(_split3(-cs), axis=1), pc_ref[...], preferred_element_type=F32))
            kaug_ref[rows, :] = placed.astype(BF16)
        carry_ref[...] = carry
    else:
        kaug_ref[...] = jnp.zeros_like(kaug_ref)
        vt_ref[...] = jnp.zeros_like(vt_ref)


def _proj(x2d, wts, *, tile, seq_len, with_cumsum):
    n = x2d.shape[0]
    nt = n // tile
    row = lambda w: pl.BlockSpec((tile, w), lambda i: (i, 0))
    tps = seq_len // tile
    kern = functools.partial(_proj_kernel, tiles_per_seq=tps, with_cumsum=with_cumsum)
    state_t = pl.BlockSpec((ATTN_W, tile), lambda i: (i // tps, i % tps))
    out_shape = (
        jax.ShapeDtypeStruct((n // seq_len * ATTN_W, seq_len), F32),
        jax.ShapeDtypeStruct((n // seq_len * ATTN_W, seq_len), F32),
        jax.ShapeDtypeStruct((n, CONV_CH), F32),
        jax.ShapeDtypeStruct((n, N_HEADS), F32),
        jax.ShapeDtypeStruct((n, ATTN_W), BF16),
        jax.ShapeDtypeStruct((n, ATTN_W), BF16),
        jax.ShapeDtypeStruct((n, ATTN_W), BF16),
        jax.ShapeDtypeStruct((n, N_HEADS * LANES), BF16),
        jax.ShapeDtypeStruct((ATTN_W, n), BF16),
    )
    out_specs = (state_t, state_t, row(CONV_CH), row(N_HEADS), row(ATTN_W), row(ATTN_W), row(ATTN_W),
                 row(N_HEADS * LANES), pl.BlockSpec((ATTN_W, tile), lambda i: (0, i)))
    in_specs = [row(D_MODEL)] + [_full_spec(w.shape) for w in wts]
    return pl.pallas_call(
        kern, grid=(nt,), in_specs=in_specs, out_specs=out_specs, out_shape=out_shape,
        scratch_shapes=[pltpu.VMEM((1, LANES), F32)],
        compiler_params=_cparams(("arbitrary",)), name="proj",
    )(x2d, *wts)


N_BIAS_TERMS = 3


def _attn_kernel(q_ref, kaug_ref, vt_ref, o_ref, acc_ref, sa_ref, sb_ref, *, tq, tk):
    qi = pl.program_id(2)
    lane = lax.broadcasted_iota(jnp.int32, (1, LANES), 1)
    q = q_ref[...].astype(F32)
    ones = jnp.where(lane < HEAD_DIM + N_BIAS_TERMS, 1.0, 0.0)
    key_row = lax.broadcasted_iota(jnp.int32, (tk, tq), 0)
    qry_col = lax.broadcasted_iota(jnp.int32, (tk, tq), 1)
    last = (qi * tq) // tk
    visible = key_row + last * tk <= qry_col + qi * tq
    qhs = []
    for hh in range(2):
        qh = q if hh == 0 else pltpu.roll(q, HEAD_DIM, axis=1)
        qhs.append(jnp.where(lane < HEAD_DIM, qh, ones).astype(BF16))
    acc_ref[...] = jnp.zeros_like(acc_ref)

    def scores(j, s_ref):
        start = pl.multiple_of(j * tk, tk)
        for hh in range(2):
            s_ref[hh] = lax.dot_general(kaug_ref[pl.ds(start, tk), hh * LANES:(hh + 1) * LANES], qhs[hh],
                                        (((1,), (1,)), ((), ())), preferred_element_type=F32)

    def consume(j, s_ref, stats, masked):
        start = pl.multiple_of(j * tk, tk)
        vt = vt_ref[:, pl.ds(start, tk)]
        new_stats = []
        for hh in range(2):
            m, l = stats[hh]
            s = jnp.where(visible, s_ref[hh], NEG_INF) if masked else s_ref[hh]
            m_new = jnp.maximum(m, jnp.max(s, axis=0, keepdims=True))
            alpha = jnp.exp(m - m_new)
            pr = jnp.exp(s - m_new)
            l = alpha * l + jnp.sum(pr, axis=0, keepdims=True)
            acc_ref[hh] = alpha * acc_ref[hh] + jnp.dot(vt, pr.astype(BF16), preferred_element_type=F32)
            new_stats.append((m_new, l))
        return tuple(new_stats)

    def pair(jj, stats):
        j = 2 * jj
        scores(j + 1, sb_ref)
        stats = consume(j, sa_ref, stats, False)
        scores(j + 2, sa_ref)
        return consume(j + 1, sb_ref, stats, False)

    def even_tail(stats):
        return consume(last, sa_ref, stats, True)

    def odd_tail(stats):
        scores(last, sb_ref)
        stats = consume(last - 1, sa_ref, stats, False)
        return consume(last, sb_ref, stats, True)

    init = (jnp.full((1, tq), NEG_INF, F32), jnp.zeros((1, tq), F32))
    scores(0, sa_ref)
    stats = lax.fori_loop(0, last // 2, pair, (init, init))
    stats = lax.cond(last % 2 == 0, even_tail, odd_tail, stats)
    row = lax.broadcasted_iota(jnp.int32, (LANES, 1), 0)
    out = jnp.where(row < HEAD_DIM, acc_ref[0] / stats[0][1], acc_ref[1] / stats[1][1])
    o_ref[...] = out.T.astype(o_ref.dtype)


def _attn(qb, kaug, vt, *, batch, seq_len, tq, tk):
    n = qb.shape[0]
    nq = seq_len // tq
    kern = functools.partial(_attn_kernel, tq=tq, tk=tk)
    return pl.pallas_call(
        kern, grid=(batch, N_HEADS // 2, nq),
        in_specs=[
            pl.BlockSpec((tq, LANES), lambda b, p, i: (b * nq + i, p)),
            pl.BlockSpec((seq_len, 2 * LANES), lambda b, p, i: (b, p)),
            pl.BlockSpec((LANES, seq_len), lambda b, p, i: (p, b)),
        ],
        out_specs=pl.BlockSpec((tq, LANES), lambda b, p, i: (b * nq + i, p)),
        out_shape=jax.ShapeDtypeStruct((n, ATTN_W), BF16),
        scratch_shapes=[pltpu.VMEM((2, LANES, tq), F32), pltpu.VMEM((2, tk, tq), F32), pltpu.VMEM((2, tk, tq), F32)],
        compiler_params=_cparams(("arbitrary", "arbitrary", "arbitrary")), name="attn",
    )(qb, kaug, vt)


def _decode_query(q_ref):
    sub = lax.broadcasted_iota(jnp.int32, (N_HEADS, ATTN_W), 0)
    lane = lax.broadcasted_iota(jnp.int32, (N_HEADS, ATTN_W), 1)
    headmask = (lane // HEAD_DIM) == sub
    qbd = jnp.where(headmask, jnp.broadcast_to(q_ref[...].astype(F32), (N_HEADS, ATTN_W)), 0.0)
    return qbd.astype(BF16), headmask


def _decode_init(lfn_ref, m_ref, l_ref, acc_ref, carry_ref):
    m_ref[...] = jnp.full_like(m_ref, NEG_INF)
    l_ref[...] = jnp.zeros_like(l_ref)
    acc_ref[...] = jnp.zeros_like(acc_ref)
    carry_ref[...] = jnp.broadcast_to(lfn_ref[...], carry_ref.shape)


def _decode_pages(q_ref, su_ref, k_refs, v_refs, lf_refs, m_ref, l_ref, acc_ref, carry_ref):
    pp = len(k_refs)
    page = lf_refs[0].shape[1]
    qbd, _ = _decode_query(q_ref)
    parts = []
    for r in lf_refs:
        parts.extend(_split3(r[...]))
    sums = jnp.dot(jnp.concatenate(parts, axis=0), su_ref[...], preferred_element_type=F32)
    carry = carry_ref[...]
    ss = []
    for i in range(pp):
        base = 3 * N_HEADS * i
        both = sums[base:base + 8] + sums[base + 8:base + 16] + sums[base + 16:base + 24]
        s = jnp.dot(qbd, k_refs[i][...].astype(BF16), preferred_element_type=F32)
        ss.append(s + both[:, :page] + carry)
        carry = carry + both[:, page:]
    carry_ref[...] = carry
    m_old = m_ref[...]
    m = m_old
    for s in ss:
        m = jnp.maximum(m, s)
    alpha = jnp.exp(m_old - m)
    prs = [jnp.exp(s - m) for s in ss]
    l = alpha * l_ref[...]
    for pr in prs:
        l = l + pr
    for h in range(N_HEADS):
        rows = slice(h * HEAD_DIM, (h + 1) * HEAD_DIM)
        acc = acc_ref[rows, :] * alpha[h:h + 1, :]
        for i in range(pp):
            acc = acc + v_refs[i][rows, :] * prs[i][h:h + 1, :]
        acc_ref[rows, :] = acc
    m_ref[...] = m
    l_ref[...] = l


def _decode_finish(q_ref, kn_ref, vn_ref, o_ref, m_ref, l_ref, acc_ref):
    page = m_ref.shape[1]
    qbd, headmask = _decode_query(q_ref)
    m = m_ref[...]
    s_self = jnp.sum(qbd.astype(F32) * kn_ref[...].astype(F32), axis=-1, keepdims=True)
    m_all = jnp.maximum(jnp.max(m, axis=-1, keepdims=True), s_self)
    w = jnp.exp(m - m_all)
    w_self = jnp.exp(s_self - m_all)
    denom = jnp.sum(l_ref[...] * w, axis=-1, keepdims=True) + w_self
    ones = jnp.ones((N_HEADS, page), BF16)
    o = jnp.zeros((N_HEADS, ATTN_W), F32)
    weighted = jnp.concatenate(
        [acc_ref[h * HEAD_DIM:(h + 1) * HEAD_DIM, :] * w[h:h + 1, :] for h in range(N_HEADS)], axis=0)
    for part in _split3(weighted):
        o = o + lax.dot_general(ones, part, (((1,), (1,)), ((), ())), preferred_element_type=F32)
    o = o + w_self * jnp.broadcast_to(vn_ref[...].astype(F32), (N_HEADS, ATTN_W))
    o = jnp.where(headmask, o / denom, 0.0)
    o_ref[...] = jnp.sum(o, axis=0, keepdims=True).astype(o_ref.dtype)


class _DecodeWork:
    def __init__(self, page_table, qb, kb, vb, lf_new, cache_kt, cache_vt, cache_lf_t, pages_per_step):
        self.db, self.n_pages = page_table.shape
        self.pp = pages_per_step
        self.steps_per_seq = self.n_pages // self.pp
        self.n_steps = self.db * self.steps_per_seq
        page = cache_lf_t.shape[2]
        newer = (lax.broadcasted_iota(jnp.int32, (page, page), 0) >
                 lax.broadcasted_iota(jnp.int32, (page, page), 1))
        su = jnp.concatenate([newer.astype(BF16), jnp.ones((page, page), BF16)], axis=1)
        self.page_table = page_table
        self.operands = (qb, kb, vb, lf_new, su, *([cache_kt] * self.pp), *([cache_vt] * self.pp),
                         *([cache_lf_t] * self.pp))
        self.page = page
        self.out_shape = jax.ShapeDtypeStruct((self.db, 1, ATTN_W), BF16)
        self.scratch = [pltpu.VMEM((N_HEADS, page), F32), pltpu.VMEM((N_HEADS, page), F32),
                        pltpu.VMEM((ATTN_W, page), F32), pltpu.VMEM((N_HEADS, page), F32)]

    def specs(self, linear_step):
        pp, page, n_pages, sps = self.pp, self.page, self.n_pages, self.steps_per_seq

        def seq_and_step(ids):
            s = jnp.minimum(linear_step(*ids), self.n_steps - 1)
            return s // sps, s % sps

        def per_seq(shape):
            return pl.BlockSpec((None,) + shape, lambda *a: (seq_and_step(a[:-1])[0], 0, 0))

        def paged(rows, i):
            def index(*a):
                b, j = seq_and_step(a[:-1])
                return (a[-1][b, n_pages - 1 - (j * pp + i)], 0, 0)
            return pl.BlockSpec((None, rows, page), index)

        in_specs = [per_seq((1, ATTN_W)), per_seq((1, ATTN_W)), per_seq((1, ATTN_W)), per_seq((N_HEADS, 1)),
                    pl.BlockSpec((page, 2 * page), lambda *a: (0, 0))]
        in_specs += [paged(ATTN_W, i) for i in range(pp)]
        in_specs += [paged(ATTN_W, i) for i in range(pp)]
        in_specs += [paged(N_HEADS, i) for i in range(pp)]
        return in_specs, per_seq((1, ATTN_W))


CONV_HALO = 32
CONV_ROWS = 32


def _ln_silu(y, g, b):
    mu = jnp.mean(y, axis=-1, keepdims=True)
    d = y - mu
    var = jnp.mean(d * d, axis=-1, keepdims=True)
    z = d * lax.rsqrt(var + EPS) * g + b
    return z * jax.nn.sigmoid(z)


def _conv_kernel(cur_ref, prev_ref, w_ref, b_ref, g_ref, lb_ref, o_ref, xs_ref, *, tiles_per_seq):
    t = cur_ref.shape[0]
    i = pl.program_id(0)
    xs_ref[0, 0:CONV_HALO, :] = jnp.where(i % tiles_per_seq == 0, 0.0, prev_ref[...])
    xs_ref[0, CONV_HALO:, :] = cur_ref[...]
    off = CONV_HALO - (CONV_W - 1)
    span = t + CONV_HALO - SUBLANES
    for r in range(1, SUBLANES):
        xs_ref[r, 0:span, :] = xs_ref[0, pl.ds(r, span), :]
    for c in range(t // CONV_ROWS):
        acc = jnp.broadcast_to(b_ref[...], (CONV_ROWS, CONV_CH))
        for w in range(CONV_W):
            r, base = (off + w) % SUBLANES, (off + w) // SUBLANES * SUBLANES
            acc = acc + xs_ref[r, pl.ds(c * CONV_ROWS + base, CONV_ROWS), :] * w_ref[w:w + 1, :]
        o_ref[c * CONV_ROWS:(c + 1) * CONV_ROWS, :] = _ln_silu(acc, g_ref[...], lb_ref[...]).astype(o_ref.dtype)


def _conv_prompt(u2d, conv_w, conv_b, ln_g, ln_b, *, tile, seq_len):
    n = u2d.shape[0]
    ratio = tile // CONV_HALO
    kern = functools.partial(_conv_kernel, tiles_per_seq=seq_len // tile)
    wts = (conv_w, conv_b, ln_g, ln_b)
    return pl.pallas_call(
        kern, grid=(n // tile,),
        in_specs=[pl.BlockSpec((tile, CONV_CH), lambda i: (i, 0)),
                  pl.BlockSpec((CONV_HALO, CONV_CH), lambda i: (jnp.maximum(i * ratio - 1, 0), 0))]
                 + [_full_spec(w.shape) for w in wts],
        out_specs=pl.BlockSpec((tile, CONV_CH), lambda i: (i, 0)),
        out_shape=jax.ShapeDtypeStruct((n, CONV_CH), BF16),
        scratch_shapes=[pltpu.VMEM((SUBLANES, tile + CONV_HALO, CONV_CH), F32)],
        compiler_params=_cparams(("arbitrary",)), name="conv_prompt",
    )(u2d, u2d, *wts)


def _conv_step_kernel(xp_ref, w_ref, b_ref, g_ref, lb_ref, o_ref):
    acc = jnp.broadcast_to(b_ref[...], o_ref.shape)
    for w in range(CONV_W):
        acc = acc + xp_ref[w] * w_ref[w:w + 1, :]
    o_ref[...] = _ln_silu(acc, g_ref[...], lb_ref[...]).astype(o_ref.dtype)


def _conv_step(xp_t, conv_w, conv_b, ln_g, ln_b):
    db = xp_t.shape[1]
    args = (xp_t, conv_w, conv_b, ln_g, ln_b)
    return pl.pallas_call(
        _conv_step_kernel, grid=(1,),
        in_specs=[_full_spec(a.shape) for a in args],
        out_specs=_full_spec((db, CONV_CH)),
        out_shape=jax.ShapeDtypeStruct((db, CONV_CH), BF16),
        compiler_params=_cparams(("arbitrary",)), name="conv_step",
    )(*args)


GATE_PITCH = PEER_KEYS + SUBLANES


def _gate_grid(tok, sub, i1_ref, i2_ref, g_ref, stage_ref):
    i1 = i1_ref[pl.ds(tok, 1), :]
    i2 = i2_ref[pl.ds(tok, 1), :]
    g = g_ref[pl.ds(tok, 1), :]
    p1 = jnp.where(sub == i1, g, 0.0).astype(BF16)
    p2 = jnp.where(sub == i2, 1.0, 0.0).astype(BF16)
    grid = lax.dot_general(p1, p2, (((1,), (1,)), ((), ())), preferred_element_type=F32)
    stage_ref[pl.ds(pl.multiple_of(tok * GATE_PITCH, SUBLANES), PEER_KEYS), :] = grid


def _gate_rows(a, t, stage_ref, o_ref):
    rows = stage_ref[pl.ds(a, t, stride=GATE_PITCH), :]
    o_ref[:, pl.ds(pl.multiple_of(a * PEER_KEYS, PEER_KEYS), PEER_KEYS)] = rows.astype(o_ref.dtype)


N_CAND = 64
ROUTE_CHAINS = 2


def _top16(chains, iota_f):
    ss = [c[0] for c in chains]
    for r in range(PEER_TOPK):
        for n, (_, val_ref, idx_ref) in enumerate(chains):
            m = jnp.max(ss[n], axis=0, keepdims=True)
            idx = jnp.min(jnp.where(ss[n] == m, iota_f, float(1 << 20)), axis=0, keepdims=True)
            val_ref[r:r + 1, :] = m
            idx_ref[r:r + 1, :] = idx
            ss[n] = jnp.where(iota_f == idx, NEG_INF, ss[n])


def _batcher_network(n):
    pairs = []
    p = 1
    while p < n:
        k = p
        while k >= 1:
            for j in range(k % p, n - k, 2 * k):
                for i in range(min(k, n - j - k)):
                    if (i + j) // (2 * p) == (i + j + k) // (2 * p):
                        pairs.append((i + j, i + j + k))
            k //= 2
        p *= 2
    return pairs


def _top16_columns(srcs, outs):
    t = srcs[0].shape[1]
    groups = PEER_KEYS // SUBLANES
    sub = lax.broadcasted_iota(jnp.int32, (SUBLANES, t), 0).astype(F32)
    vals = [[s[SUBLANES * j:SUBLANES * (j + 1), :] for j in range(groups)] for s in srcs]
    idxs = [[sub + float(SUBLANES * j) for j in range(groups)] for _ in srcs]
    for a, b in _batcher_network(groups):
        for v, ix in zip(vals, idxs):
            swap = v[b] > v[a]
            v[a], v[b] = jnp.maximum(v[a], v[b]), jnp.minimum(v[a], v[b])
            ix[a], ix[b] = jnp.where(swap, ix[b], ix[a]), jnp.where(swap, ix[a], ix[b])
    flag = jnp.zeros((SUBLANES, t), F32)
    for v in vals:
        for d in range(groups - 1):
            flag = jnp.maximum(flag, jnp.where(v[d] == v[d + 1], 1.0, 0.0))
    for r in range(PEER_TOPK):
        for v, ix, (val_ref, idx_ref) in zip(vals, idxs, outs):
            m = jnp.max(v[0], axis=0, keepdims=True)
            idx = jnp.min(jnp.where(v[0] == m, ix[0], float(1 << 20)), axis=0, keepdims=True)
            val_ref[r:r + 1, :] = m
            idx_ref[r:r + 1, :] = idx
            won = ix[0] == idx
            for d in range(PEER_TOPK - 1 - r):
                v[d] = jnp.where(won, v[d + 1], v[d])
                ix[d] = jnp.where(won, ix[d + 1], ix[d])
    return flag


def _candidate_ids(t):
    row = lax.broadcasted_iota(jnp.int32, (N_CAND, t), 0)
    r1 = jnp.where(row < 16, 0, jnp.where(row < 40, (row - 8) // 8, jnp.where(row < 56, (row - 24) // 4, row - 48)))
    r2 = jnp.where(row < 16, row, jnp.where(row < 40, row % 8, jnp.where(row < 56, row % 4, 0)))
    return (r1 * PEER_TOPK + r2).astype(F32)


def _route_kernel(x_ref, a_ref, c_ref, woa_ref, woc_ref, g2_ref, wq_ref, sk_ref,
                  y_ref, h_ref, gd_ref,
                  sc_ref, tv_ref, ti_ref, cand_ref, bs_ref, o1_ref, o2_ref, og_ref, p1_ref, p2_ref, pg_ref, stage_ref):
    t = x_ref.shape[0]

    @pl.when(pl.program_id(0) == 0)
    def _():
        p1_ref[...] = jnp.zeros_like(p1_ref)
        p2_ref[...] = jnp.zeros_like(p2_ref)
        pg_ref[...] = jnp.zeros_like(pg_ref)

    y = (x_ref[...] + jnp.dot(a_ref[...], woa_ref[...], preferred_element_type=F32)
         + jnp.dot(c_ref[...], woc_ref[...], preferred_element_type=F32))
    y_ref[...] = y
    ms = jnp.mean(y * y, axis=-1, keepdims=True)
    h = (y * lax.rsqrt(ms + EPS) * g2_ref[...]).astype(BF16)
    h_ref[...] = h
    q = jnp.dot(h, wq_ref[...], preferred_element_type=F32).astype(BF16)
    for hp in range(2 * PEER_HEADS):
        sc_ref[hp] = lax.dot_general(sk_ref[hp], q[:, hp * PEER_KEYS:(hp + 1) * PEER_KEYS],
                                     (((1,), (1,)), ((), ())), preferred_element_type=F32)

    key_iota = lax.broadcasted_iota(jnp.int32, (PEER_KEYS, t), 0).astype(F32)

    pick_sub = lax.broadcasted_iota(jnp.int32, (PEER_KEYS, PEER_HEADS * PEER_TOPK), 0).astype(F32)
    tokens_per_trip = t // (2 * PEER_HEADS // ROUTE_CHAINS)

    def stage1(it, c):
        hps = [it * ROUTE_CHAINS + n for n in range(ROUTE_CHAINS)]
        outs = [(tv_ref.at[hp], ti_ref.at[hp]) for hp in hps]
        unordered_ties = _top16_columns([sc_ref[hp] for hp in hps], outs)
        for u in range(tokens_per_trip):
            _gate_grid(it * tokens_per_trip + u, pick_sub, p1_ref, p2_ref, pg_ref, stage_ref)

        @pl.when(jnp.max(unordered_ties) > 0.0)
        def _():
            _top16([(sc_ref[hp], val_ref, idx_ref) for hp, (val_ref, idx_ref) in zip(hps, outs)], key_iota)

        return c

    lax.fori_loop(0, 2 * PEER_HEADS // ROUTE_CHAINS, stage1, 0)

    cand_id = _candidate_ids(t)
    rank_iota = lax.broadcasted_iota(jnp.int32, (PEER_TOPK, t), 0).astype(F32)
    low4 = lax.broadcasted_iota(jnp.int32, (SUBLANES, t), 0) < 4

    def stage2(it, c):
        heads = [it * ROUTE_CHAINS + n for n in range(ROUTE_CHAINS)]
        ss, i1s, i2s = [], [], []
        for n, hd in enumerate(heads):
            v1 = tv_ref[2 * hd]
            v2 = tv_ref[2 * hd + 1]
            i1s.append(ti_ref[2 * hd])
            i2s.append(ti_ref[2 * hd + 1])
            v2lo = v2[0:SUBLANES, :]
            v2q = jnp.where(low4, v2lo, pltpu.roll(v2lo, 4, axis=0))
            cand_ref[n, 0:16, :] = v1[0:1, :] + v2
            for r1 in (1, 2, 3):
                cand_ref[n, 8 + 8 * r1:16 + 8 * r1, :] = v1[r1:r1 + 1, :] + v2lo
            cand_ref[n, 40:48, :] = jnp.where(low4, v1[4:5, :], v1[5:6, :]) + v2q
            cand_ref[n, 48:56, :] = jnp.where(low4, v1[6:7, :], v1[7:8, :]) + v2q
            cand_ref[n, 56:64, :] = v1[8:16, :] + v2[0:1, :]
            ss.append(cand_ref[n])
        for r in range(PEER_TOPK):
            for n, hd in enumerate(heads):
                base = pl.multiple_of(hd * PEER_TOPK, PEER_TOPK)
                m = jnp.max(ss[n], axis=0, keepdims=True)
                cid = jnp.min(jnp.where(ss[n] == m, cand_id, float(1 << 20)), axis=0, keepdims=True)
                ss[n] = jnp.where(cand_id == cid, NEG_INF, ss[n])
                r1 = jnp.floor(cid * (1.0 / PEER_TOPK))
                r2 = cid - r1 * PEER_TOPK
                bs_ref[n, r:r + 1, :] = m
                o1_ref[pl.ds(base + r, 1), :] = jnp.max(jnp.where(rank_iota == r1, i1s[n], -1.0), axis=0, keepdims=True)
                o2_ref[pl.ds(base + r, 1), :] = jnp.max(jnp.where(rank_iota == r2, i2s[n], -1.0), axis=0, keepdims=True)
        for n, hd in enumerate(heads):
            base = pl.multiple_of(hd * PEER_TOPK, PEER_TOPK)
            bs = bs_ref[n]
            e = jnp.exp(bs - bs[0:1, :])
            og_ref[pl.ds(base, PEER_TOPK), :] = e / jnp.sum(e, axis=0, keepdims=True)
        rows_per_trip = PEER_KEYS // (PEER_HEADS // ROUTE_CHAINS)
        for u in range(rows_per_trip):
            _gate_rows(it * rows_per_trip + u, t, stage_ref, gd_ref)
        return c

    lax.fori_loop(0, PEER_HEADS // ROUTE_CHAINS, stage2, 0)
    p1_ref[...] = o1_ref[...].T
    p2_ref[...] = o2_ref[...].T
    pg_ref[...] = og_ref[...].T


def _route(x2d, a2d, c2d, wts, *, tile):
    n = x2d.shape[0]
    nt = n // tile
    hk = PEER_HEADS * PEER_TOPK
    row = lambda w: pl.BlockSpec((tile, w), lambda i: (jnp.minimum(i, nt - 1), 0))
    out_shape = (jax.ShapeDtypeStruct((n, D_MODEL), F32),
                 jax.ShapeDtypeStruct((n, D_MODEL), BF16),
                 jax.ShapeDtypeStruct((n, N_EXPERTS), BF16))
    return pl.pallas_call(
        _route_kernel, grid=(nt + 1,),
        in_specs=[row(D_MODEL), row(ATTN_W), row(CONV_CH)] + [_full_spec(w.shape) for w in wts],
        out_specs=(row(D_MODEL), row(D_MODEL),
                   pl.BlockSpec((tile, N_EXPERTS), lambda i: (jnp.maximum(i - 1, 0), 0))),
        out_shape=out_shape,
        scratch_shapes=[pltpu.VMEM((2 * PEER_HEADS, PEER_KEYS, tile), F32),
                        pltpu.VMEM((2 * PEER_HEADS, PEER_TOPK, tile), F32),
                        pltpu.VMEM((2 * PEER_HEADS, PEER_TOPK, tile), F32),
                        pltpu.VMEM((ROUTE_CHAINS, N_CAND, tile), F32),
                        pltpu.VMEM((ROUTE_CHAINS, PEER_TOPK, tile), F32),
                        pltpu.VMEM((hk, tile), F32), pltpu.VMEM((hk, tile), F32), pltpu.VMEM((hk, tile), F32),
                        pltpu.VMEM((tile, hk), F32), pltpu.VMEM((tile, hk), F32), pltpu.VMEM((tile, hk), F32),
                        pltpu.VMEM((tile * GATE_PITCH, PEER_KEYS), F32)],
        compiler_params=_cparams(("arbitrary",)), name="route",
    )(x2d, a2d, c2d, *wts)


def _experts_kernel(pt_ref, h_ref, g_ref, u_ref, v_ref, y_ref, *rest, decode):
    del pt_ref
    if decode is None:
        o_ref, acc_ref, even_ref, odd_ref = rest
    else:
        nd = 5 + 3 * decode.pp
        q_ref, kn_ref, vn_ref, lfn_ref, su_ref = rest[:5]
        k_refs = rest[5:5 + decode.pp]
        v_refs = rest[5 + decode.pp:5 + 2 * decode.pp]
        lf_refs = rest[5 + 2 * decode.pp:nd]
        o_ref, od_ref, acc_ref, even_ref, odd_ref, m_ref, l_ref, dacc_ref, carry_ref = rest[nd:]
        state = (m_ref, l_ref, dacc_ref, carry_ref)
        step = pl.program_id(0) * pl.num_programs(1) + pl.program_id(1)
        active = step < decode.n_steps
        seq_step = jnp.minimum(step, decode.n_steps - 1) % decode.steps_per_seq
    e = pl.program_id(1)

    @pl.when(e == 0)
    def _():
        acc_ref[...] = jnp.zeros_like(acc_ref)
        odd_ref[...] = jnp.zeros_like(odd_ref)

    if decode is not None:
        pl.when(active & (seq_step == 0))(lambda: _decode_init(lfn_ref, *state))

    def work(cur_ref, prev_ref):
        if decode is not None:
            _decode_pages(q_ref, su_ref, k_refs, v_refs, lf_refs, *state)
        h = h_ref[...]
        for c in range(u_ref.shape[0] // MXU_TILE):
            cols = slice(c * MXU_TILE, (c + 1) * MXU_TILE)
            s = lax.dot_general(h, u_ref[cols, :], (((1,), (1,)), ((), ())), preferred_element_type=F32)
            act = 0.5 * s * (1.0 + lax.erf(s * math.sqrt(0.5)))
            cur_ref[:, cols] = (g_ref[:, cols].astype(F32) * act).astype(BF16)
        prev = prev_ref[...]
        for c in range(v_ref.shape[1] // MXU_TILE):
            cols = slice(c * MXU_TILE, (c + 1) * MXU_TILE)
            acc_ref[:, cols] += jnp.dot(prev, v_ref[:, cols], preferred_element_type=F32)

    pl.when(e % 2 == 0)(lambda: work(even_ref, odd_ref))
    pl.when(e % 2 == 1)(lambda: work(odd_ref, even_ref))

    if decode is not None:
        pl.when(active & (seq_step == decode.steps_per_seq - 1))(
            lambda: _decode_finish(q_ref, kn_ref, vn_ref, od_ref, m_ref, l_ref, dacc_ref))

    @pl.when(e == pl.num_programs(1) - 1)
    def _():
        o_ref[...] = y_ref[...] + acc_ref[...]


def _experts(h2d, gdense, u_tab, v_tab, y2d, *, tile, eblk, decode=None):
    n = h2d.shape[0]
    nb = N_EXPERTS // eblk
    grid = (n // tile, nb + 1)
    scored = lambda e: jnp.minimum(e, nb - 1)
    folded = lambda e: jnp.maximum(e - 1, 0)
    in_specs = [pl.BlockSpec((tile, D_MODEL), lambda i, e, pt: (i, 0)),
                pl.BlockSpec((tile, eblk), lambda i, e, pt: (i, scored(e))),
                pl.BlockSpec((eblk, D_MODEL), lambda i, e, pt: (scored(e), 0)),
                pl.BlockSpec((eblk, D_MODEL), lambda i, e, pt: (folded(e), 0)),
                pl.BlockSpec((tile, D_MODEL), lambda i, e, pt: (i, 0))]
    out_specs = [pl.BlockSpec((tile, D_MODEL), lambda i, e, pt: (i, 0))]
    out_shape = [jax.ShapeDtypeStruct((n, D_MODEL), F32)]
    scratch = [pltpu.VMEM((tile, D_MODEL), F32), pltpu.VMEM((tile, eblk), BF16), pltpu.VMEM((tile, eblk), BF16)]
    operands = [h2d, gdense, u_tab, v_tab, y2d]
    if decode is None:
        prefetch = jnp.zeros((1, 1), jnp.int32)
    else:
        assert decode.n_steps <= grid[0] * grid[1], "not enough host steps for the paged attention"
        prefetch = decode.page_table
        dec_in, dec_out = decode.specs(lambda i, e: i * grid[1] + e)
        in_specs += dec_in
        out_specs.append(dec_out)
        out_shape.append(decode.out_shape)
        scratch += decode.scratch
        operands += list(decode.operands)
    grid_spec = pltpu.PrefetchScalarGridSpec(
        num_scalar_prefetch=1, grid=grid, in_specs=in_specs, out_specs=out_specs, scratch_shapes=scratch)
    outs = pl.pallas_call(
        functools.partial(_experts_kernel, decode=decode), grid_spec=grid_spec, out_shape=out_shape,
        compiler_params=_cparams(("arbitrary", "arbitrary")), name="experts",
    )(prefetch, *operands)
    return outs[0] if decode is None else tuple(outs)


PROJ_TILE = 512
ATTN_TILE = 256
ATTN_KEY_TILE = 512
CONV_TILE = 256
ROUTE_TILE = 128
EXPERT_TILE = 1024
EXPERT_BLOCK = 512
DECODE_PAGES_PER_STEP = 8


def _finish(x2d, a2d, c2d, route_wts, u_tab, v_tab, *, expert_tile, decode=None):
    n = x2d.shape[0]
    y, h2, gd = _route(x2d, a2d, c2d, route_wts, tile=min(ROUTE_TILE, n))
    return _experts(h2, gd, u_tab, v_tab, y, tile=expert_tile, eblk=EXPERT_BLOCK, decode=decode)


def kernel(x_prompt, x_sample, cache_k, cache_v, cache_logf, state_conv, page_table, norm1_g, w_in, b_forget,
           q_gain, k_gain, conv_w, conv_b, conv_ln_g, conv_ln_b, w_out, norm2_g, peer_wq, peer_subkeys,
           peer_u, peer_v):
    assert w_in.shape[0] == 1, "single layer"
    batch, seq_len, _ = x_prompt.shape
    db = x_sample.shape[0]
    n_pool, page = cache_k.shape[1], cache_k.shape[2]

    w = w_in[0].astype(BF16)
    w_qkv = w[:, :3 * ATTN_W]
    w_ag = w[:, 3 * ATTN_W:3 * ATTN_W + 2 * CONV_CH]
    w_fg = jnp.pad(w[:, 3 * ATTN_W + 2 * CONV_CH:], ((0, 0), (0, LANES - N_HEADS)))
    b_fg = jnp.pad(b_forget[0].reshape(1, N_HEADS), ((0, 0), (0, LANES - N_HEADS)))
    qg = jnp.tile(q_gain[0], N_HEADS).reshape(1, ATTN_W)
    kg = jnp.tile(k_gain[0], N_HEADS).reshape(1, ATTN_W)
    gi = lax.broadcasted_iota(jnp.int32, (ATTN_W, ATTN_W), 0) // HEAD_DIM
    gj = lax.broadcasted_iota(jnp.int32, (ATTN_W, ATTN_W), 1) // HEAD_DIM
    bd = (gi == gj).astype(BF16)
    tri = (lax.broadcasted_iota(jnp.int32, (LANES, LANES), 1) <=
           lax.broadcasted_iota(jnp.int32, (LANES, LANES), 0)).astype(BF16)
    src = lax.broadcasted_iota(jnp.int32, (ATTN_W, N_HEADS * LANES), 0)
    dst = lax.broadcasted_iota(jnp.int32, (ATTN_W, N_HEADS * LANES), 1)
    place_k = (dst == (src // HEAD_DIM) * LANES + src % HEAD_DIM).astype(BF16)
    src = lax.broadcasted_iota(jnp.int32, (N_BIAS_TERMS * LANES, N_HEADS * LANES), 0)
    dst = lax.broadcasted_iota(jnp.int32, (N_BIAS_TERMS * LANES, N_HEADS * LANES), 1)
    place_c = ((src % LANES < N_HEADS) & (dst == (src % LANES) * LANES + HEAD_DIM + src // LANES)).astype(BF16)
    proj_wts = (norm1_g[0].reshape(1, D_MODEL), w_qkv, w_ag, w_fg, b_fg, qg, kg, bd, tri, place_k, place_c)
    conv_wts = (conv_w[0], conv_b[0].reshape(1, CONV_CH), conv_ln_g[0].reshape(1, CONV_CH),
                conv_ln_b[0].reshape(1, CONV_CH))
    wo = w_out[0].astype(BF16)
    sk = peer_subkeys[0].astype(BF16).reshape(2 * PEER_HEADS, PEER_KEYS, PEER_KEYS)
    route_wts = (wo[:ATTN_W], wo[ATTN_W:], norm2_g[0].reshape(1, D_MODEL), peer_wq[0].astype(BF16), sk)
    u_tab = peer_u[0].astype(BF16)
    v_tab = peer_v[0].astype(BF16)

    xp2 = x_prompt.reshape(batch * seq_len, D_MODEL)
    k_p, v_p, u_p, lf_p, qb, _, _, kaug, vt = _proj(
        xp2, proj_wts, tile=PROJ_TILE, seq_len=seq_len, with_cumsum=True)
    xs2 = x_sample.reshape(db, D_MODEL)
    k_s, v_s, u_s, lf_s, qsb, ksb, vsb, _, _ = _proj(xs2, proj_wts, tile=db, seq_len=db, with_cumsum=False)

    lf_t = cache_logf[0].transpose(0, 2, 1)
    row = lambda a: a.reshape(db, 1, ATTN_W)
    transposed = lambda c: c[0].transpose(0, 2, 3, 1).reshape(n_pool, ATTN_W, page)
    decode = _DecodeWork(page_table, row(qsb), row(ksb), row(vsb), lf_s.reshape(db, N_HEADS, 1),
                         transposed(cache_k), transposed(cache_v), lf_t, DECODE_PAGES_PER_STEP)

    a_p = _attn(qb, kaug, vt, batch=batch, seq_len=seq_len, tq=ATTN_TILE, tk=ATTN_KEY_TILE)
    c_p = _conv_prompt(u_p, *conv_wts, tile=CONV_TILE, seq_len=seq_len)
    y_p, a_s = _finish(xp2, a_p, c_p, route_wts, u_tab, v_tab, expert_tile=EXPERT_TILE, decode=decode)

    window = jnp.concatenate([state_conv[0], u_s[:, None, :]], axis=1)
    c_s = _conv_step(window.transpose(1, 0, 2), *conv_wts)
    y_s = _finish(xs2, a_s.reshape(db, ATTN_W), c_s, route_wts, u_tab, v_tab, expert_tile=db)

    tail = CONV_W - 1
    state = lambda t, seqs, toks: t.reshape(seqs, N_HEADS, HEAD_DIM, toks).transpose(0, 3, 1, 2)[None]
    return (y_p.reshape(batch, seq_len, D_MODEL),
            y_s.reshape(db, 1, D_MODEL),
            state(k_p, batch, seq_len),
            state(v_p, batch, seq_len),
            lf_p.reshape(1, batch, seq_len, N_HEADS),
            u_p.reshape(batch, seq_len, CONV_CH)[:, seq_len - tail:][None],
            state(k_s, 1, db).reshape(1, db, 1, N_HEADS, HEAD_DIM),
            state(v_s, 1, db).reshape(1, db, 1, N_HEADS, HEAD_DIM),
            lf_s.reshape(1, db, 1, N_HEADS),
            window[:, 1:][None])
```

```python
import functools
import math

import jax
import jax.numpy as jnp
from jax import lax
from jax.experimental import pallas as pl
from jax.experimental.pallas import tpu as pltpu

F32 = jnp.float32
BF16 = jnp.bfloat16

D_MODEL = 1024
HEAD_DIM = 64
N_HEADS = 8
ATTN_W = N_HEADS * HEAD_DIM
CONV_CH = D_MODEL - ATTN_W
CONV_W = 31
PEER_HEADS = 8
PEER_KEYS = 128
PEER_TOPK = 16
N_EXPERTS = PEER_KEYS * PEER_KEYS
EPS = 1e-6
LANES = 128
SUBLANES = 8
MXU_TILE = 256
VMEM_LIMIT = 56 * 1024 * 1024
NEG_INF = float("-inf")


def _cparams(sem):
    return pltpu.CompilerParams(dimension_semantics=sem, vmem_limit_bytes=VMEM_LIMIT)


def _split3(x):
    p1 = x.astype(BF16)
    r1 = x - p1.astype(F32)
    p2 = r1.astype(BF16)
    p3 = (r1 - p2.astype(F32)).astype(BF16)
    return p1, p2, p3


def _full_spec(shape):
    return pl.BlockSpec(shape, lambda *_: (0,) * len(shape))


def _group_rms(t, bd, gain):
    sq = t * t
    hi = sq.astype(BF16)
    lo = (sq - hi.astype(F32)).astype(BF16)
    ss = jnp.dot(hi, bd, preferred_element_type=F32) + jnp.dot(lo, bd, preferred_element_type=F32)
    return t * lax.rsqrt(ss * (1.0 / HEAD_DIM) + EPS) * gain


def _proj_kernel(x_ref, g1_ref, wqkv_ref, wag_ref, wfg_ref, bf_ref, qg_ref, kg_ref, bd_ref, tri_ref, pk_ref, pc_ref,
                 k_ref, v_ref, u_ref, lf_ref, qb_ref, kb_ref, vb_ref, kaug_ref, vt_ref,
                 carry_ref, *, tiles_per_seq, with_cumsum):
    t = x_ref.shape[0]
    x = x_ref[...]
    ms = jnp.mean(x * x, axis=-1, keepdims=True)
    h = (x * lax.rsqrt(ms + EPS) * g1_ref[...]).astype(BF16)

    qkv = jnp.dot(h, wqkv_ref[...], preferred_element_type=F32)
    bd = bd_ref[...]
    q = _group_rms(qkv[:, :ATTN_W], bd, qg_ref[...])
    k = _group_rms(qkv[:, ATTN_W:2 * ATTN_W], bd, kg_ref[...])
    v = qkv[:, 2 * ATTN_W:]
    v_t = v.T
    k_ref[...] = k.T
    v_ref[...] = v_t
    qb_ref[...] = (q * (HEAD_DIM ** -0.5)).astype(BF16)
    kb_ref[...] = k.astype(BF16)
    vb_ref[...] = v.astype(BF16)

    ag = jnp.dot(h, wag_ref[...], preferred_element_type=F32)
    u_ref[...] = ag[:, :CONV_CH] * jax.nn.sigmoid(ag[:, CONV_CH:])

    z = jnp.dot(h, wfg_ref[...], preferred_element_type=F32) + bf_ref[...]
    logf = jnp.minimum(z, 0.0) - jnp.log1p(jnp.exp(-jnp.abs(z)))
    lf_ref[...] = logf[:, :N_HEADS]

    if with_cumsum:
        vt_ref[...] = v_t.astype(BF16)
        i = pl.program_id(0)
        lane = lax.broadcasted_iota(jnp.int32, (1, LANES), 1)
        carry = jnp.where(i % tiles_per_seq == 0, 0.0, carry_ref[...])
        tri = tri_ref[...]
        for sb in range(t // LANES):
            rows = slice(sb * LANES, (sb + 1) * LANES)
            blk = jnp.where(lane < N_HEADS, logf[rows], 0.0)
            p1, p2, p3 = _split3(blk)
            cs = (jnp.dot(tri, p1, preferred_element_type=F32)
                  + jnp.dot(tri, p2, preferred_element_type=F32)
                  + jnp.dot(tri, p3, preferred_element_type=F32)) + carry
            carry = cs[LANES - 1:LANES, :]
            placed = (jnp.dot(k[rows].astype(BF16), pk_ref[...], preferred_element_type=F32)
                      + jnp.dot(jnp.concatenate(_split3(-cs), axis=1), pc_ref[...], preferred_element_type=F32))
            kaug_ref[rows, :] = placed.astype(BF16)
        carry_ref[...] = carry
    else:
        kaug_ref[...] = jnp.zeros_like(kaug_ref)
        vt_ref[...] = jnp.zeros_like(vt_ref)


def _proj(x2d, wts, *, tile, seq_len, with_cumsum):
    n = x2d.shape[0]
    nt = n // tile
    row = lambda w: pl.BlockSpec((tile, w), lambda i: (i, 0))
    tps = seq_len // tile
    kern = functools.partial(_proj_kernel, tiles_per_seq=tps, with_cumsum=with_cumsum)
    state_t = pl.BlockSpec((ATTN_W, tile), lambda i: (i // tps, i % tps))
    out_shape = (
        jax.ShapeDtypeStruct((n // seq_len * ATTN_W, seq_len), F32),
        jax.ShapeDtypeStruct((n // seq_len * ATTN_W, seq_len), F32),
        jax.ShapeDtypeStruct((n, CONV_CH), F32),
        jax.ShapeDtypeStruct((n, N_HEADS), F32),
        jax.ShapeDtypeStruct((n, ATTN_W), BF16),
        jax.ShapeDtypeStruct((n, ATTN_W), BF16),
        jax.ShapeDtypeStruct((n, ATTN_W), BF16),
        jax.ShapeDtypeStruct((n, N_HEADS * LANES), BF16),
        jax.ShapeDtypeStruct((ATTN_W, n), BF16),
    )
    out_specs = (state_t, state_t, row(CONV_CH), row(N_HEADS), row(ATTN_W), row(ATTN_W), row(ATTN_W),
                 row(N_HEADS * LANES), pl.BlockSpec((ATTN_W, tile), lambda i: (0, i)))
    in_specs = [row(D_MODEL)] + [_full_spec(w.shape) for w in wts]
    return pl.pallas_call(
        kern, grid=(nt,), in_specs=in_specs, out_specs=out_specs, out_shape=out_shape,
        scratch_shapes=[pltpu.VMEM((1, LANES), F32)],
        compiler_params=_cparams(("arbitrary",)), name="proj",
    )(x2d, *wts)


N_BIAS_TERMS = 3


def _attn_kernel(q_ref, kaug_ref, vt_ref, o_ref, acc_ref, sa_ref, sb_ref, *, tq, tk):
    qi = pl.program_id(2)
    lane = lax.broadcasted_iota(jnp.int32, (1, LANES), 1)
    q = q_ref[...].astype(F32)
    ones = jnp.where(lane < HEAD_DIM + N_BIAS_TERMS, 1.0, 0.0)
    key_row = lax.broadcasted_iota(jnp.int32, (tk, tq), 0)
    qry_col = lax.broadcasted_iota(jnp.int32, (tk, tq), 1)
    last = (qi * tq) // tk
    visible = key_row + last * tk <= qry_col + qi * tq
    qhs = []
    for hh in range(2):
        qh = q if hh == 0 else pltpu.roll(q, HEAD_DIM, axis=1)
        qhs.append(jnp.where(lane < HEAD_DIM, qh, ones).astype(BF16))
    acc_ref[...] = jnp.zeros_like(acc_ref)

    def scores(j, s_ref):
        start = pl.multiple_of(j * tk, tk)
        for hh in range(2):
            s_ref[hh] = lax.dot_general(kaug_ref[pl.ds(start, tk), hh * LANES:(hh + 1) * LANES], qhs[hh],
                                        (((1,), (1,)), ((), ())), preferred_element_type=F32)

    def consume(j, s_ref, stats, masked):
        start = pl.multiple_of(j * tk, tk)
        vt = vt_ref[:, pl.ds(start, tk)]
        new_stats = []
        for hh in range(2):
            m, l = stats[hh]
            s = jnp.where(visible, s_ref[hh], NEG_INF) if masked else s_ref[hh]
            m_new = jnp.maximum(m, jnp.max(s, axis=0, keepdims=True))
            alpha = jnp.exp(m - m_new)
            pr = jnp.exp(s - m_new)
            l = alpha * l + jnp.sum(pr, axis=0, keepdims=True)
            acc_ref[hh] = alpha * acc_ref[hh] + jnp.dot(vt, pr.astype(BF16), preferred_element_type=F32)
            new_stats.append((m_new, l))
        return tuple(new_stats)

    def pair(jj, stats):
        j = 2 * jj
        scores(j + 1, sb_ref)
        stats = consume(j, sa_ref, stats, False)
        scores(j + 2, sa_ref)
        return consume(j + 1, sb_ref, stats, False)

    def even_tail(stats):
        return consume(last, sa_ref, stats, True)

    def odd_tail(stats):
        scores(last, sb_ref)
        stats = consume(last - 1, sa_ref, stats, False)
        return consume(last, sb_ref, stats, True)

    init = (jnp.full((1, tq), NEG_INF, F32), jnp.zeros((1, tq), F32))
    scores(0, sa_ref)
    stats = lax.fori_loop(0, last // 2, pair, (init, init))
    stats = lax.cond(last % 2 == 0, even_tail, odd_tail, stats)
    row = lax.broadcasted_iota(jnp.int32, (LANES, 1), 0)
    out = jnp.where(row < HEAD_DIM, acc_ref[0] / stats[0][1], acc_ref[1] / stats[1][1])
    o_ref[...] = out.T.astype(o_ref.dtype)


def _attn(qb, kaug, vt, *, batch, seq_len, tq, tk):
    n = qb.shape[0]
    nq = seq_len // tq
    kern = functools.partial(_attn_kernel, tq=tq, tk=tk)
    return pl.pallas_call(
        kern, grid=(batch, N_HEADS // 2, nq),
        in_specs=[
            pl.BlockSpec((tq, LANES), lambda b, p, i: (b * nq + i, p)),
            pl.BlockSpec((seq_len, 2 * LANES), lambda b, p, i: (b, p)),
            pl.BlockSpec((LANES, seq_len), lambda b, p, i: (p, b)),
        ],
        out_specs=pl.BlockSpec((tq, LANES), lambda b, p, i: (b * nq + i, p)),
        out_shape=jax.ShapeDtypeStruct((n, ATTN_W), BF16),
        scratch_shapes=[pltpu.VMEM((2, LANES, tq), F32), pltpu.VMEM((2, tk, tq), F32), pltpu.VMEM((2, tk, tq), F32)],
        compiler_params=_cparams(("arbitrary", "arbitrary", "arbitrary")), name="attn",
    )(qb, kaug, vt)


def _decode_query(q_ref):
    sub = lax.broadcasted_iota(jnp.int32, (N_HEADS, ATTN_W), 0)
    lane = lax.broadcasted_iota(jnp.int32, (N_HEADS, ATTN_W), 1)
    headmask = (lane // HEAD_DIM) == sub
    qbd = jnp.where(headmask, jnp.broadcast_to(q_ref[...].astype(F32), (N_HEADS, ATTN_W)), 0.0)
    return qbd.astype(BF16), headmask


def _decode_init(lfn_ref, m_ref, l_ref, acc_ref, carry_ref):
    m_ref[...] = jnp.full_like(m_ref, NEG_INF)
    l_ref[...] = jnp.zeros_like(l_ref)
    acc_ref[...] = jnp.zeros_like(acc_ref)
    carry_ref[...] = jnp.broadcast_to(lfn_ref[...], carry_ref.shape)


def _decode_pages(q_ref, su_ref, k_refs, v_refs, lf_refs, m_ref, l_ref, acc_ref, carry_ref):
    pp = len(k_refs)
    page = lf_refs[0].shape[1]
    qbd, _ = _decode_query(q_ref)
    parts = []
    for r in lf_refs:
        parts.extend(_split3(r[...]))
    sums = jnp.dot(jnp.concatenate(parts, axis=0), su_ref[...], preferred_element_type=F32)
    carry = carry_ref[...]
    ss = []
    for i in range(pp):
        base = 3 * N_HEADS * i
        both = sums[base:base + 8] + sums[base + 8:base + 16] + sums[base + 16:base + 24]
        s = jnp.dot(qbd, k_refs[i][...].astype(BF16), preferred_element_type=F32)
        ss.append(s + both[:, :page] + carry)
        carry = carry + both[:, page:]
    carry_ref[...] = carry
    m_old = m_ref[...]
    m = m_old
    for s in ss:
        m = jnp.maximum(m, s)
    alpha = jnp.exp(m_old - m)
    prs = [jnp.exp(s - m) for s in ss]
    l = alpha * l_ref[...]
    for pr in prs:
        l = l + pr
    for h in range(N_HEADS):
        rows = slice(h * HEAD_DIM, (h + 1) * HEAD_DIM)
        acc = acc_ref[rows, :] * alpha[h:h + 1, :]
        for i in range(pp):
            acc = acc + v_refs[i][rows, :] * prs[i][h:h + 1, :]
        acc_ref[rows, :] = acc
    m_ref[...] = m
    l_ref[...] = l


def _decode_finish(q_ref, kn_ref, vn_ref, o_ref, m_ref, l_ref, acc_ref):
    page = m_ref.shape[1]
    qbd, headmask = _decode_query(q_ref)
    m = m_ref[...]
    s_self = jnp.sum(qbd.astype(F32) * kn_ref[...].astype(F32), axis=-1, keepdims=True)
    m_all = jnp.maximum(jnp.max(m, axis=-1, keepdims=True), s_self)
    w = jnp.exp(m - m_all)
    w_self = jnp.exp(s_self - m_all)
    denom = jnp.sum(l_ref[...] * w, axis=-1, keepdims=True) + w_self
    ones = jnp.ones((N_HEADS, page), BF16)
    o = jnp.zeros((N_HEADS, ATTN_W), F32)
    weighted = jnp.concatenate(
        [acc_ref[h * HEAD_DIM:(h + 1) * HEAD_DIM, :] * w[h:h + 1, :] for h in range(N_HEADS)], axis=0)
    for part in _split3(weighted):
        o = o + lax.dot_general(ones, part, (((1,), (1,)), ((), ())), preferred_element_type=F32)
    o = o + w_self * jnp.broadcast_to(vn_ref[...].astype(F32), (N_HEADS, ATTN_W))
    o = jnp.where(headmask, o / denom, 0.0)
    o_ref[...] = jnp.sum(o, axis=0, keepdims=True).astype(o_ref.dtype)


class _DecodeWork:
    def __init__(self, page_table, qb, kb, vb, lf_new, cache_kt, cache_vt, cache_lf_t, pages_per_step):
        self.db, self.n_pages = page_table.shape
        self.pp = pages_per_step
        self.steps_per_seq = self.n_pages // self.pp
        self.n_steps = self.db * self.steps_per_seq
        page = cache_lf_t.shape[2]
        newer = (lax.broadcasted_iota(jnp.int32, (page, page), 0) >
                 lax.broadcasted_iota(jnp.int32, (page, page), 1))
        su = jnp.concatenate([newer.astype(BF16), jnp.ones((page, page), BF16)], axis=1)
        self.page_table = page_table
        self.operands = (qb, kb, vb, lf_new, su, cache_kt, cache_vt, cache_lf_t)
        self.page = page
        self.out_shape = jax.ShapeDtypeStruct((self.db, 1, ATTN_W), BF16)
        self.scratch = [pltpu.VMEM((N_HEADS, page), F32), pltpu.VMEM((N_HEADS, page), F32),
                        pltpu.VMEM((ATTN_W, page), F32), pltpu.VMEM((N_HEADS, page), F32),
                        pltpu.VMEM((2, self.pp, ATTN_W, page), F32), pltpu.VMEM((2, self.pp, ATTN_W, page), F32),
                        pltpu.VMEM((2, self.pp, N_HEADS, page), F32), pltpu.SemaphoreType.DMA((2, 3))]

    def specs(self, linear_step):
        page, sps = self.page, self.steps_per_seq

        def per_seq(shape):
            seq = lambda ids: jnp.minimum(linear_step(*ids), self.n_steps - 1) // sps
            return pl.BlockSpec((None,) + shape, lambda *a: (seq(a[:-1]), 0, 0))

        in_specs = [per_seq((1, ATTN_W)), per_seq((1, ATTN_W)), per_seq((1, ATTN_W)), per_seq((N_HEADS, 1)),
                    pl.BlockSpec((page, 2 * page), lambda *a: (0, 0))]
        in_specs += [pl.BlockSpec(memory_space=pl.ANY)] * 3
        return in_specs, per_seq((1, ATTN_W))

    def copies(self, pt_ref, step, slot, kt_hbm, vt_hbm, lf_hbm, kbuf, vbuf, lfbuf, sem):
        b = step // self.steps_per_seq
        j = step % self.steps_per_seq
        out = []
        for i in range(self.pp):
            pg = pt_ref[b, self.n_pages - 1 - (j * self.pp + i)]
            out.append(pltpu.make_async_copy(kt_hbm.at[pg], kbuf.at[slot, i], sem.at[slot, 0]))
            out.append(pltpu.make_async_copy(vt_hbm.at[pg], vbuf.at[slot, i], sem.at[slot, 1]))
            out.append(pltpu.make_async_copy(lf_hbm.at[pg], lfbuf.at[slot, i], sem.at[slot, 2]))
        return out


CONV_HALO = 32
CONV_ROWS = 32


def _ln_silu(y, g, b):
    mu = jnp.mean(y, axis=-1, keepdims=True)
    d = y - mu
    var = jnp.mean(d * d, axis=-1, keepdims=True)
    z = d * lax.rsqrt(var + EPS) * g + b
    return z * jax.nn.sigmoid(z)


def _conv_kernel(cur_ref, prev_ref, w_ref, b_ref, g_ref, lb_ref, o_ref, xs_ref, *, tiles_per_seq):
    t = cur_ref.shape[0]
    i = pl.program_id(0)
    xs_ref[0, 0:CONV_HALO, :] = jnp.where(i % tiles_per_seq == 0, 0.0, prev_ref[...])
    xs_ref[0, CONV_HALO:, :] = cur_ref[...]
    off = CONV_HALO - (CONV_W - 1)
    span = t + CONV_HALO - SUBLANES
    for r in range(1, SUBLANES):
        xs_ref[r, 0:span, :] = xs_ref[0, pl.ds(r, span), :]
    for c in range(t // CONV_ROWS):
        acc = jnp.broadcast_to(b_ref[...], (CONV_ROWS, CONV_CH))
        for w in range(CONV_W):
            r, base = (off + w) % SUBLANES, (off + w) // SUBLANES * SUBLANES
            acc = acc + xs_ref[r, pl.ds(c * CONV_ROWS + base, CONV_ROWS), :] * w_ref[w:w + 1, :]
        o_ref[c * CONV_ROWS:(c + 1) * CONV_ROWS, :] = _ln_silu(acc, g_ref[...], lb_ref[...]).astype(o_ref.dtype)


def _conv_prompt(u2d, conv_w, conv_b, ln_g, ln_b, *, tile, seq_len):
    n = u2d.shape[0]
    ratio = tile // CONV_HALO
    kern = functools.partial(_conv_kernel, tiles_per_seq=seq_len // tile)
    wts = (conv_w, conv_b, ln_g, ln_b)
    return pl.pallas_call(
        kern, grid=(n // tile,),
        in_specs=[pl.BlockSpec((tile, CONV_CH), lambda i: (i, 0)),
                  pl.BlockSpec((CONV_HALO, CONV_CH), lambda i: (jnp.maximum(i * ratio - 1, 0), 0))]
                 + [_full_spec(w.shape) for w in wts],
        out_specs=pl.BlockSpec((tile, CONV_CH), lambda i: (i, 0)),
        out_shape=jax.ShapeDtypeStruct((n, CONV_CH), BF16),
        scratch_shapes=[pltpu.VMEM((SUBLANES, tile + CONV_HALO, CONV_CH), F32)],
        compiler_params=_cparams(("arbitrary",)), name="conv_prompt",
    )(u2d, u2d, *wts)


def _conv_step_kernel(xp_ref, w_ref, b_ref, g_ref, lb_ref, o_ref):
    acc = jnp.broadcast_to(b_ref[...], o_ref.shape)
    for w in range(CONV_W):
        acc = acc + xp_ref[w] * w_ref[w:w + 1, :]
    o_ref[...] = _ln_silu(acc, g_ref[...], lb_ref[...]).astype(o_ref.dtype)


def _conv_step(xp_t, conv_w, conv_b, ln_g, ln_b):
    db = xp_t.shape[1]
    args = (xp_t, conv_w, conv_b, ln_g, ln_b)
    return pl.pallas_call(
        _conv_step_kernel, grid=(1,),
        in_specs=[_full_spec(a.shape) for a in args],
        out_specs=_full_spec((db, CONV_CH)),
        out_shape=jax.ShapeDtypeStruct((db, CONV_CH), BF16),
        compiler_params=_cparams(("arbitrary",)), name="conv_step",
    )(*args)


GATE_PITCH = PEER_KEYS + SUBLANES


def _gate_grid(tok, sub, i1_ref, i2_ref, g_ref, stage_ref):
    i1 = i1_ref[pl.ds(tok, 1), :]
    i2 = i2_ref[pl.ds(tok, 1), :]
    g = g_ref[pl.ds(tok, 1), :]
    p1 = jnp.where(sub == i1, g, 0.0).astype(BF16)
    p2 = jnp.where(sub == i2, 1.0, 0.0).astype(BF16)
    grid = lax.dot_general(p1, p2, (((1,), (1,)), ((), ())), preferred_element_type=F32)
    stage_ref[pl.ds(pl.multiple_of(tok * GATE_PITCH, SUBLANES), PEER_KEYS), :] = grid


def _gate_rows(a, t, stage_ref, o_ref):
    rows = stage_ref[pl.ds(a, t, stride=GATE_PITCH), :]
    o_ref[:, pl.ds(pl.multiple_of(a * PEER_KEYS, PEER_KEYS), PEER_KEYS)] = rows.astype(o_ref.dtype)


N_CAND = 64
ROUTE_CHAINS = 2


def _top16(chains, iota_f):
    ss = [c[0] for c in chains]
    for r in range(PEER_TOPK):
        for n, (_, val_ref, idx_ref) in enumerate(chains):
            m = jnp.max(ss[n], axis=0, keepdims=True)
            idx = jnp.min(jnp.where(ss[n] == m, iota_f, float(1 << 20)), axis=0, keepdims=True)
            val_ref[r:r + 1, :] = m
            idx_ref[r:r + 1, :] = idx
            ss[n] = jnp.where(iota_f == idx, NEG_INF, ss[n])


def _batcher_network(n):
    pairs = []
    p = 1
    while p < n:
        k = p
        while k >= 1:
            for j in range(k % p, n - k, 2 * k):
                for i in range(min(k, n - j - k)):
                    if (i + j) // (2 * p) == (i + j + k) // (2 * p):
                        pairs.append((i + j, i + j + k))
            k //= 2
        p *= 2
    return pairs


def _top16_columns(srcs, outs):
    t = srcs[0].shape[1]
    groups = PEER_KEYS // SUBLANES
    sub = lax.broadcasted_iota(jnp.int32, (SUBLANES, t), 0).astype(F32)
    vals = [[s[SUBLANES * j:SUBLANES * (j + 1), :] for j in range(groups)] for s in srcs]
    idxs = [[sub + float(SUBLANES * j) for j in range(groups)] for _ in srcs]
    for a, b in _batcher_network(groups):
        for v, ix in zip(vals, idxs):
            swap = v[b] > v[a]
            v[a], v[b] = jnp.maximum(v[a], v[b]), jnp.minimum(v[a], v[b])
            ix[a], ix[b] = jnp.where(swap, ix[b], ix[a]), jnp.where(swap, ix[a], ix[b])
    flag = jnp.zeros((SUBLANES, t), F32)
    for v in vals:
        for d in range(groups - 1):
            flag = jnp.maximum(flag, jnp.where(v[d] == v[d + 1], 1.0, 0.0))
    for r in range(PEER_TOPK):
        for v, ix, (val_ref, idx_ref) in zip(vals, idxs, outs):
            m = jnp.max(v[0], axis=0, keepdims=True)
            idx = jnp.min(jnp.where(v[0] == m, ix[0], float(1 << 20)), axis=0, keepdims=True)
            val_ref[r:r + 1, :] = m
            idx_ref[r:r + 1, :] = idx
            won = ix[0] == idx
            for d in range(PEER_TOPK - 1 - r):
                v[d] = jnp.where(won, v[d + 1], v[d])
                ix[d] = jnp.where(won, ix[d + 1], ix[d])
    return flag


def _candidate_ids(t):
    row = lax.broadcasted_iota(jnp.int32, (N_CAND, t), 0)
    r1 = jnp.where(row < 16, 0, jnp.where(row < 40, (row - 8) // 8, jnp.where(row < 56, (row - 24) // 4, row - 48)))
    r2 = jnp.where(row < 16, row, jnp.where(row < 40, row % 8, jnp.where(row < 56, row % 4, 0)))
    return (r1 * PEER_TOPK + r2).astype(F32)


def _route_kernel(x_ref, a_ref, c_ref, woa_ref, woc_ref, g2_ref, wq_ref, sk_ref,
                  y_ref, h_ref, gd_ref,
                  sc_ref, tv_ref, ti_ref, cand_ref, bs_ref, o1_ref, o2_ref, og_ref, p1_ref, p2_ref, pg_ref, stage_ref):
    t = x_ref.shape[0]

    @pl.when(pl.program_id(0) == 0)
    def _():
        p1_ref[...] = jnp.zeros_like(p1_ref)
        p2_ref[...] = jnp.zeros_like(p2_ref)
        pg_ref[...] = jnp.zeros_like(pg_ref)

    y = (x_ref[...] + jnp.dot(a_ref[...], woa_ref[...], preferred_element_type=F32)
         + jnp.dot(c_ref[...], woc_ref[...], preferred_element_type=F32))
    y_ref[...] = y
    ms = jnp.mean(y * y, axis=-1, keepdims=True)
    h = (y * lax.rsqrt(ms + EPS) * g2_ref[...]).astype(BF16)
    h_ref[...] = h
    q = jnp.dot(h, wq_ref[...], preferred_element_type=F32).astype(BF16)
    for hp in range(2 * PEER_HEADS):
        sc_ref[hp] = lax.dot_general(sk_ref[hp], q[:, hp * PEER_KEYS:(hp + 1) * PEER_KEYS],
                                     (((1,), (1,)), ((), ())), preferred_element_type=F32)

    key_iota = lax.broadcasted_iota(jnp.int32, (PEER_KEYS, t), 0).astype(F32)

    pick_sub = lax.broadcasted_iota(jnp.int32, (PEER_KEYS, PEER_HEADS * PEER_TOPK), 0).astype(F32)
    tokens_per_trip = t // (2 * PEER_HEADS // ROUTE_CHAINS)

    def stage1(it, c):
        hps = [it * ROUTE_CHAINS + n for n in range(ROUTE_CHAINS)]
        outs = [(tv_ref.at[hp], ti_ref.at[hp]) for hp in hps]
        unordered_ties = _top16_columns([sc_ref[hp] for hp in hps], outs)
        for u in range(tokens_per_trip):
            _gate_grid(it * tokens_per_trip + u, pick_sub, p1_ref, p2_ref, pg_ref, stage_ref)

        @pl.when(jnp.max(unordered_ties) > 0.0)
        def _():
            _top16([(sc_ref[hp], val_ref, idx_ref) for hp, (val_ref, idx_ref) in zip(hps, outs)], key_iota)

        return c

    lax.fori_loop(0, 2 * PEER_HEADS // ROUTE_CHAINS, stage1, 0)

    cand_id = _candidate_ids(t)
    rank_iota = lax.broadcasted_iota(jnp.int32, (PEER_TOPK, t), 0).astype(F32)
    low4 = lax.broadcasted_iota(jnp.int32, (SUBLANES, t), 0) < 4

    def stage2(it, c):
        heads = [it * ROUTE_CHAINS + n for n in range(ROUTE_CHAINS)]
        ss, i1s, i2s = [], [], []
        for n, hd in enumerate(heads):
            v1 = tv_ref[2 * hd]
            v2 = tv_ref[2 * hd + 1]
            i1s.append(ti_ref[2 * hd])
            i2s.append(ti_ref[2 * hd + 1])
            v2lo = v2[0:SUBLANES, :]
            v2q = jnp.where(low4, v2lo, pltpu.roll(v2lo, 4, axis=0))
            cand_ref[n, 0:16, :] = v1[0:1, :] + v2
            for r1 in (1, 2, 3):
                cand_ref[n, 8 + 8 * r1:16 + 8 * r1, :] = v1[r1:r1 + 1, :] + v2lo
            cand_ref[n, 40:48, :] = jnp.where(low4, v1[4:5, :], v1[5:6, :]) + v2q
            cand_ref[n, 48:56, :] = jnp.where(low4, v1[6:7, :], v1[7:8, :]) + v2q
            cand_ref[n, 56:64, :] = v1[8:16, :] + v2[0:1, :]
            ss.append(cand_ref[n])
        for r in range(PEER_TOPK):
            for n, hd in enumerate(heads):
                base = pl.multiple_of(hd * PEER_TOPK, PEER_TOPK)
                m = jnp.max(ss[n], axis=0, keepdims=True)
                cid = jnp.min(jnp.where(ss[n] == m, cand_id, float(1 << 20)), axis=0, keepdims=True)
                ss[n] = jnp.where(cand_id == cid, NEG_INF, ss[n])
                r1 = jnp.floor(cid * (1.0 / PEER_TOPK))
                r2 = cid - r1 * PEER_TOPK
                bs_ref[n, r:r + 1, :] = m
                o1_ref[pl.ds(base + r, 1), :] = jnp.max(jnp.where(rank_iota == r1, i1s[n], -1.0), axis=0, keepdims=True)
                o2_ref[pl.ds(base + r, 1), :] = jnp.max(jnp.where(rank_iota == r2, i2s[n], -1.0), axis=0, keepdims=True)
        for n, hd in enumerate(heads):
            base = pl.multiple_of(hd * PEER_TOPK, PEER_TOPK)
            bs = bs_ref[n]
            e = jnp.exp(bs - bs[0:1, :])
            og_ref[pl.ds(base, PEER_TOPK), :] = e / jnp.sum(e, axis=0, keepdims=True)
        rows_per_trip = PEER_KEYS // (PEER_HEADS // ROUTE_CHAINS)
        for u in range(rows_per_trip):
            _gate_rows(it * rows_per_trip + u, t, stage_ref, gd_ref)
        return c

    lax.fori_loop(0, PEER_HEADS // ROUTE_CHAINS, stage2, 0)
    p1_ref[...] = o1_ref[...].T
    p2_ref[...] = o2_ref[...].T
    pg_ref[...] = og_ref[...].T


def _route(x2d, a2d, c2d, wts, *, tile):
    n = x2d.shape[0]
    nt = n // tile
    hk = PEER_HEADS * PEER_TOPK
    row = lambda w: pl.BlockSpec((tile, w), lambda i: (jnp.minimum(i, nt - 1), 0))
    out_shape = (jax.ShapeDtypeStruct((n, D_MODEL), F32),
                 jax.ShapeDtypeStruct((n, D_MODEL), BF16),
                 jax.ShapeDtypeStruct((n, N_EXPERTS), BF16))
    return pl.pallas_call(
        _route_kernel, grid=(nt + 1,),
        in_specs=[row(D_MODEL), row(ATTN_W), row(CONV_CH)] + [_full_spec(w.shape) for w in wts],
        out_specs=(row(D_MODEL), row(D_MODEL),
                   pl.BlockSpec((tile, N_EXPERTS), lambda i: (jnp.maximum(i - 1, 0), 0))),
        out_shape=out_shape,
        scratch_shapes=[pltpu.VMEM((2 * PEER_HEADS, PEER_KEYS, tile), F32),
                        pltpu.VMEM((2 * PEER_HEADS, PEER_TOPK, tile), F32),
                        pltpu.VMEM((2 * PEER_HEADS, PEER_TOPK, tile), F32),
                        pltpu.VMEM((ROUTE_CHAINS, N_CAND, tile), F32),
                        pltpu.VMEM((ROUTE_CHAINS, PEER_TOPK, tile), F32),
                        pltpu.VMEM((hk, tile), F32), pltpu.VMEM((hk, tile), F32), pltpu.VMEM((hk, tile), F32),
                        pltpu.VMEM((tile, hk), F32), pltpu.VMEM((tile, hk), F32), pltpu.VMEM((tile, hk), F32),
                        pltpu.VMEM((tile * GATE_PITCH, PEER_KEYS), F32)],
        compiler_params=_cparams(("arbitrary",)), name="route",
    )(x2d, a2d, c2d, *wts)


def _experts_kernel(pt_ref, h_ref, g_ref, u_ref, v_ref, y_ref, *rest, decode):
    if decode is None:
        o_ref, acc_ref, even_ref, odd_ref = rest
    else:
        q_ref, kn_ref, vn_ref, lfn_ref, su_ref, kt_hbm, vt_hbm, lf_hbm = rest[:8]
        (o_ref, od_ref, acc_ref, even_ref, odd_ref, m_ref, l_ref, dacc_ref, carry_ref,
         kbuf, vbuf, lfbuf, sem) = rest[8:]
        state = (m_ref, l_ref, dacc_ref, carry_ref)
        n_host = pl.num_programs(0) * pl.num_programs(1)
        step = pl.program_id(0) * pl.num_programs(1) + pl.program_id(1)
        active = step < decode.n_steps
        cur = jnp.minimum(step, decode.n_steps - 1)
        nxt = jnp.minimum(step + 1, decode.n_steps - 1)
        seq_step = cur % decode.steps_per_seq
        slot = step % 2
        copies = functools.partial(decode.copies, pt_ref, kt_hbm=kt_hbm, vt_hbm=vt_hbm, lf_hbm=lf_hbm,
                                   kbuf=kbuf, vbuf=vbuf, lfbuf=lfbuf, sem=sem)
    e = pl.program_id(1)

    @pl.when(e == 0)
    def _():
        acc_ref[...] = jnp.zeros_like(acc_ref)
        odd_ref[...] = jnp.zeros_like(odd_ref)

    if decode is not None:
        pl.when(active & (seq_step == 0))(lambda: _decode_init(lfn_ref, *state))

        @pl.when(step == 0)
        def _():
            for c in copies(0, 0):
                c.start()

    def work(cur_ref, prev_ref):
        if decode is not None:
            for c in copies(nxt, 1 - slot):
                c.start()
            for c in copies(cur, slot):
                c.wait()
            pages = lambda buf: [buf.at[slot, i] for i in range(decode.pp)]
            _decode_pages(q_ref, su_ref, pages(kbuf), pages(vbuf), pages(lfbuf), *state)
        h = h_ref[...]
        for c in range(u_ref.shape[0] // MXU_TILE):
            cols = slice(c * MXU_TILE, (c + 1) * MXU_TILE)
            s = lax.dot_general(h, u_ref[cols, :], (((1,), (1,)), ((), ())), preferred_element_type=F32)
            act = 0.5 * s * (1.0 + lax.erf(s * math.sqrt(0.5)))
            cur_ref[:, cols] = (g_ref[:, cols].astype(F32) * act).astype(BF16)
        prev = prev_ref[...]
        for c in range(v_ref.shape[1] // MXU_TILE):
            cols = slice(c * MXU_TILE, (c + 1) * MXU_TILE)
            acc_ref[:, cols] += jnp.dot(prev, v_ref[:, cols], preferred_element_type=F32)

    pl.when(e % 2 == 0)(lambda: work(even_ref, odd_ref))
    pl.when(e % 2 == 1)(lambda: work(odd_ref, even_ref))

    if decode is not None:
        pl.when(active & (seq_step == decode.steps_per_seq - 1))(
            lambda: _decode_finish(q_ref, kn_ref, vn_ref, od_ref, m_ref, l_ref, dacc_ref))

        @pl.when(step == n_host - 1)
        def _():
            for c in copies(nxt, 1 - slot):
                c.wait()

    @pl.when(e == pl.num_programs(1) - 1)
    def _():
        o_ref[...] = y_ref[...] + acc_ref[...]


def _experts(h2d, gdense, u_tab, v_tab, y2d, *, tile, eblk, decode=None):
    n = h2d.shape[0]
    nb = N_EXPERTS // eblk
    grid = (n // tile, nb + 1)
    scored = lambda e: jnp.minimum(e, nb - 1)
    folded = lambda e: jnp.maximum(e - 1, 0)
    in_specs = [pl.BlockSpec((tile, D_MODEL), lambda i, e, pt: (i, 0)),
                pl.BlockSpec((tile, eblk), lambda i, e, pt: (i, scored(e))),
                pl.BlockSpec((eblk, D_MODEL), lambda i, e, pt: (scored(e), 0)),
                pl.BlockSpec((eblk, D_MODEL), lambda i, e, pt: (folded(e), 0)),
                pl.BlockSpec((tile, D_MODEL), lambda i, e, pt: (i, 0))]
    out_specs = [pl.BlockSpec((tile, D_MODEL), lambda i, e, pt: (i, 0))]
    out_shape = [jax.ShapeDtypeStruct((n, D_MODEL), F32)]
    scratch = [pltpu.VMEM((tile, D_MODEL), F32), pltpu.VMEM((tile, eblk), BF16), pltpu.VMEM((tile, eblk), BF16)]
    operands = [h2d, gdense, u_tab, v_tab, y2d]
    if decode is None:
        prefetch = jnp.zeros((1, 1), jnp.int32)
    else:
        assert decode.n_steps <= grid[0] * grid[1], "not enough host steps for the paged attention"
        prefetch = decode.page_table
        dec_in, dec_out = decode.specs(lambda i, e: i * grid[1] + e)
        in_specs += dec_in
        out_specs.append(dec_out)
        out_shape.append(decode.out_shape)
        scratch += decode.scratch
        operands += list(decode.operands)
    grid_spec = pltpu.PrefetchScalarGridSpec(
        num_scalar_prefetch=1, grid=grid, in_specs=in_specs, out_specs=out_specs, scratch_shapes=scratch)
    outs = pl.pallas_call(
        functools.partial(_experts_kernel, decode=decode), grid_spec=grid_spec, out_shape=out_shape,
        compiler_params=_cparams(("arbitrary", "arbitrary")), name="experts",
    )(prefetch, *operands)
    return outs[0] if decode is None else tuple(outs)


PROJ_TILE = 512
ATTN_TILE = 256
ATTN_KEY_TILE = 512
CONV_TILE = 256
ROUTE_TILE = 128
EXPERT_TILE = 1024
EXPERT_BLOCK = 512
DECODE_PAGES_PER_STEP = 8


def _finish(x2d, a2d, c2d, route_wts, u_tab, v_tab, *, expert_tile, decode=None):
    n = x2d.shape[0]
    y, h2, gd = _route(x2d, a2d, c2d, route_wts, tile=min(ROUTE_TILE, n))
    return _experts(h2, gd, u_tab, v_tab, y, tile=expert_tile, eblk=EXPERT_BLOCK, decode=decode)


def kernel(x_prompt, x_sample, cache_k, cache_v, cache_logf, state_conv, page_table, norm1_g, w_in, b_forget,
           q_gain, k_gain, conv_w, conv_b, conv_ln_g, conv_ln_b, w_out, norm2_g, peer_wq, peer_subkeys,
           peer_u, peer_v):
    assert w_in.shape[0] == 1, "single layer"
    batch, seq_len, _ = x_prompt.shape
    db = x_sample.shape[0]
    n_pool, page = cache_k.shape[1], cache_k.shape[2]

    w = w_in[0].astype(BF16)
    w_qkv = w[:, :3 * ATTN_W]
    w_ag = w[:, 3 * ATTN_W:3 * ATTN_W + 2 * CONV_CH]
    w_fg = jnp.pad(w[:, 3 * ATTN_W + 2 * CONV_CH:], ((0, 0), (0, LANES - N_HEADS)))
    b_fg = jnp.pad(b_forget[0].reshape(1, N_HEADS), ((0, 0), (0, LANES - N_HEADS)))
    qg = jnp.tile(q_gain[0], N_HEADS).reshape(1, ATTN_W)
    kg = jnp.tile(k_gain[0], N_HEADS).reshape(1, ATTN_W)
    gi = lax.broadcasted_iota(jnp.int32, (ATTN_W, ATTN_W), 0) // HEAD_DIM
    gj = lax.broadcasted_iota(jnp.int32, (ATTN_W, ATTN_W), 1) // HEAD_DIM
    bd = (gi == gj).astype(BF16)
    tri = (lax.broadcasted_iota(jnp.int32, (LANES, LANES), 1) <=
           lax.broadcasted_iota(jnp.int32, (LANES, LANES), 0)).astype(BF16)
    src = lax.broadcasted_iota(jnp.int32, (ATTN_W, N_HEADS * LANES), 0)
    dst = lax.broadcasted_iota(jnp.int32, (ATTN_W, N_HEADS * LANES), 1)
    place_k = (dst == (src // HEAD_DIM) * LANES + src % HEAD_DIM).astype(BF16)
    src = lax.broadcasted_iota(jnp.int32, (N_BIAS_TERMS * LANES, N_HEADS * LANES), 0)
    dst = lax.broadcasted_iota(jnp.int32, (N_BIAS_TERMS * LANES, N_HEADS * LANES), 1)
    place_c = ((src % LANES < N_HEADS) & (dst == (src % LANES) * LANES + HEAD_DIM + src // LANES)).astype(BF16)
    proj_wts = (norm1_g[0].reshape(1, D_MODEL), w_qkv, w_ag, w_fg, b_fg, qg, kg, bd, tri, place_k, place_c)
    conv_wts = (conv_w[0], conv_b[0].reshape(1, CONV_CH), conv_ln_g[0].reshape(1, CONV_CH),
                conv_ln_b[0].reshape(1, CONV_CH))
    wo = w_out[0].astype(BF16)
    sk = peer_subkeys[0].astype(BF16).reshape(2 * PEER_HEADS, PEER_KEYS, PEER_KEYS)
    route_wts = (wo[:ATTN_W], wo[ATTN_W:], norm2_g[0].reshape(1, D_MODEL), peer_wq[0].astype(BF16), sk)
    u_tab = peer_u[0].astype(BF16)
    v_tab = peer_v[0].astype(BF16)

    xp2 = x_prompt.reshape(batch * seq_len, D_MODEL)
    k_p, v_p, u_p, lf_p, qb, _, _, kaug, vt = _proj(
        xp2, proj_wts, tile=PROJ_TILE, seq_len=seq_len, with_cumsum=True)
    xs2 = x_sample.reshape(db, D_MODEL)
    k_s, v_s, u_s, lf_s, qsb, ksb, vsb, _, _ = _proj(xs2, proj_wts, tile=db, seq_len=db, with_cumsum=False)

    lf_t = cache_logf[0].transpose(0, 2, 1)
    row = lambda a: a.reshape(db, 1, ATTN_W)
    transposed = lambda c: c[0].transpose(0, 2, 3, 1).reshape(n_pool, ATTN_W, page)
    decode = _DecodeWork(page_table, row(qsb), row(ksb), row(vsb), lf_s.reshape(db, N_HEADS, 1),
                         transposed(cache_k), transposed(cache_v), lf_t, DECODE_PAGES_PER_STEP)

    a_p = _attn(qb, kaug, vt, batch=batch, seq_len=seq_len, tq=ATTN_TILE, tk=ATTN_KEY_TILE)
    c_p = _conv_prompt(u_p, *conv_wts, tile=CONV_TILE, seq_len=seq_len)
    y_p, a_s = _finish(xp2, a_p, c_p, route_wts, u_tab, v_tab, expert_tile=EXPERT_TILE, decode=decode)

    window = jnp.concatenate([state_conv[0], u_s[:, None, :]], axis=1)
    c_s = _conv_step(window.transpose(1, 0, 2), *conv_wts)
    y_s = _finish(xs2, a_s.reshape(db, ATTN_W), c_s, route_wts, u_tab, v_tab, expert_tile=db)

    tail = CONV_W - 1
    state = lambda t, seqs, toks: t.reshape(seqs, N_HEADS, HEAD_DIM, toks).transpose(0, 3, 1, 2)[None]
    return (y_p.reshape(batch, seq_len, D_MODEL),
            y_s.reshape(db, 1, D_MODEL),
            state(k_p, batch, seq_len),
            state(v_p, batch, seq_len),
            lf_p.reshape(1, batch, seq_len, N_HEADS),
            u_p.reshape(batch, seq_len, CONV_CH)[:, seq_len - tail:][None],
            state(k_s, 1, db).reshape(1, db, 1, N_HEADS, HEAD_DIM),
            state(v_s, 1, db).reshape(1, db, 1, N_HEADS, HEAD_DIM),
            lf_s.reshape(1, db, 1, N_HEADS),
            window[:, 1:][None])
```

```python
import functools
import math

import jax
import jax.numpy as jnp
from jax import lax
from jax.experimental import pallas as pl
from jax.experimental.pallas import tpu as pltpu

F32 = jnp.float32
BF16 = jnp.bfloat16

D_MODEL = 1024
HEAD_DIM = 64
N_HEADS = 8
ATTN_W = N_HEADS * HEAD_DIM
CONV_CH = D_MODEL - ATTN_W
CONV_W = 31
PEER_HEADS = 8
PEER_KEYS = 128
PEER_TOPK = 16
N_EXPERTS = PEER_KEYS * PEER_KEYS
EPS = 1e-6
LANES = 128
SUBLANES = 8
MXU_TILE = 256
VMEM_LIMIT = 56 * 1024 * 1024
NEG_INF = float("-inf")


def _cparams(sem):
    return pltpu.CompilerParams(dimension_semantics=sem, vmem_limit_bytes=VMEM_LIMIT)


def _split3(x):
    p1 = x.astype(BF16)
    r1 = x - p1.astype(F32)
    p2 = r1.astype(BF16)
    p3 = (r1 - p2.astype(F32)).astype(BF16)
    return p1, p2, p3


def _full_spec(shape):
    return pl.BlockSpec(shape, lambda *_: (0,) * len(shape))


def _group_rms(t, bd, gain):
    sq = t * t
    hi = sq.astype(BF16)
    lo = (sq - hi.astype(F32)).astype(BF16)
    ss = jnp.dot(hi, bd, preferred_element_type=F32) + jnp.dot(lo, bd, preferred_element_type=F32)
    return t * lax.rsqrt(ss * (1.0 / HEAD_DIM) + EPS) * gain


def _proj_kernel(x_ref, g1_ref, wqkv_ref, wag_ref, wfg_ref, bf_ref, qg_ref, kg_ref, bd_ref, tri_ref, pk_ref, pc_ref,
                 k_ref, v_ref, u_ref, lf_ref, qb_ref, kb_ref, vb_ref, kaug_ref, vt_ref,
                 carry_ref, *, tiles_per_seq, with_cumsum):
    t = x_ref.shape[0]
    x = x_ref[...]
    ms = jnp.mean(x * x, axis=-1, keepdims=True)
    h = (x * lax.rsqrt(ms + EPS) * g1_ref[...]).astype(BF16)

    qkv = jnp.dot(h, wqkv_ref[...], preferred_element_type=F32)
    bd = bd_ref[...]
    q = _group_rms(qkv[:, :ATTN_W], bd, qg_ref[...])
    k = _group_rms(qkv[:, ATTN_W:2 * ATTN_W], bd, kg_ref[...])
    v = qkv[:, 2 * ATTN_W:]
    v_t = v.T
    k_ref[...] = k.T
    v_ref[...] = v_t
    qb_ref[...] = (q * (HEAD_DIM ** -0.5)).astype(BF16)
    kb_ref[...] = k.astype(BF16)
    vb_ref[...] = v.astype(BF16)

    ag = jnp.dot(h, wag_ref[...], preferred_element_type=F32)
    u_ref[...] = ag[:, :CONV_CH] * jax.nn.sigmoid(ag[:, CONV_CH:])

    z = jnp.dot(h, wfg_ref[...], preferred_element_type=F32) + bf_ref[...]
    logf = jnp.minimum(z, 0.0) - jnp.log1p(jnp.exp(-jnp.abs(z)))
    lf_ref[...] = logf[:, :N_HEADS]

    if with_cumsum:
        vt_ref[...] = v_t.astype(BF16)
        i = pl.program_id(0)
        lane = lax.broadcasted_iota(jnp.int32, (1, LANES), 1)
        carry = jnp.where(i % tiles_per_seq == 0, 0.0, carry_ref[...])
        tri = tri_ref[...]
        for sb in range(t // LANES):
            rows = slice(sb * LANES, (sb + 1) * LANES)
            blk = jnp.where(lane < N_HEADS, logf[rows], 0.0)
            p1, p2, p3 = _split3(blk)
            cs = (jnp.dot(tri, p1, preferred_element_type=F32)
                  + jnp.dot(tri, p2, preferred_element_type=F32)
                  + jnp.dot(tri, p3, preferred_element_type=F32)) + carry
            carry = cs[LANES - 1:LANES, :]
            placed = (jnp.dot(k[rows].astype(BF16), pk_ref[...], preferred_element_type=F32)
                      + jnp.dot(jnp.concatenate(_split3(-cs), axis=1), pc_ref[...], preferred_element_type=F32))
            kaug_ref[rows, :] = placed.astype(BF16)
        carry_ref[...] = carry
    else:
        kaug_ref[...] = jnp.zeros_like(kaug_ref)
        vt_ref[...] = jnp.zeros_like(vt_ref)


def _proj(x2d, wts, *, tile, seq_len, with_cumsum):
    n = x2d.shape[0]
    nt = n // tile
    row = lambda w: pl.BlockSpec((tile, w), lambda i: (i, 0))
    tps = seq_len // tile
    kern = functools.partial(_proj_kernel, tiles_per_seq=tps, with_cumsum=with_cumsum)
    state_t = pl.BlockSpec((ATTN_W, tile), lambda i: (i // tps, i % tps))
    out_shape = (
        jax.ShapeDtypeStruct((n // seq_len * ATTN_W, seq_len), F32),
        jax.ShapeDtypeStruct((n // seq_len * ATTN_W, seq_len), F32),
        jax.ShapeDtypeStruct((n, CONV_CH), F32),
        jax.ShapeDtypeStruct((n, N_HEADS), F32),
        jax.ShapeDtypeStruct((n, ATTN_W), BF16),
        jax.ShapeDtypeStruct((n, ATTN_W), BF16),
        jax.ShapeDtypeStruct((n, ATTN_W), BF16),
        jax.ShapeDtypeStruct((n, N_HEADS * LANES), BF16),
        jax.ShapeDtypeStruct((ATTN_W, n), BF16),
    )
    out_specs = (state_t, state_t, row(CONV_CH), row(N_HEADS), row(ATTN_W), row(ATTN_W), row(ATTN_W),
                 row(N_HEADS * LANES), pl.BlockSpec((ATTN_W, tile), lambda i: (0, i)))
    in_specs = [row(D_MODEL)] + [_full_spec(w.shape) for w in wts]
    return pl.pallas_call(
        kern, grid=(nt,), in_specs=in_specs, out_specs=out_specs, out_shape=out_shape,
        scratch_shapes=[pltpu.VMEM((1, LANES), F32)],
        compiler_params=_cparams(("arbitrary",)), name="proj",
    )(x2d, *wts)


N_BIAS_TERMS = 3


def _attn_kernel(q_ref, kaug_ref, vt_ref, o_ref, acc_ref, sa_ref, sb_ref, *, tq, tk):
    qi = pl.program_id(2)
    lane = lax.broadcasted_iota(jnp.int32, (1, LANES), 1)
    q = q_ref[...].astype(F32)
    ones = jnp.where(lane < HEAD_DIM + N_BIAS_TERMS, 1.0, 0.0)
    key_row = lax.broadcasted_iota(jnp.int32, (tk, tq), 0)
    qry_col = lax.broadcasted_iota(jnp.int32, (tk, tq), 1)
    last = (qi * tq) // tk
    visible = key_row + last * tk <= qry_col + qi * tq
    qhs = []
    for hh in range(2):
        qh = q if hh == 0 else pltpu.roll(q, HEAD_DIM, axis=1)
        qhs.append(jnp.where(lane < HEAD_DIM, qh, ones).astype(BF16))
    acc_ref[...] = jnp.zeros_like(acc_ref)

    def scores(j, s_ref):
        start = pl.multiple_of(j * tk, tk)
        for hh in range(2):
            s_ref[hh] = lax.dot_general(kaug_ref[pl.ds(start, tk), hh * LANES:(hh + 1) * LANES], qhs[hh],
                                        (((1,), (1,)), ((), ())), preferred_element_type=F32)

    def consume(j, s_ref, stats, masked):
        start = pl.multiple_of(j * tk, tk)
        vt = vt_ref[:, pl.ds(start, tk)]
        new_stats = []
        for hh in range(2):
            m, l = stats[hh]
            s = jnp.where(visible, s_ref[hh], NEG_INF) if masked else s_ref[hh]
            m_new = jnp.maximum(m, jnp.max(s, axis=0, keepdims=True))
            alpha = jnp.exp(m - m_new)
            pr = jnp.exp(s - m_new)
            l = alpha * l + jnp.sum(pr, axis=0, keepdims=True)
            acc_ref[hh] = alpha * acc_ref[hh] + jnp.dot(vt, pr.astype(BF16), preferred_element_type=F32)
            new_stats.append((m_new, l))
        return tuple(new_stats)

    def pair(jj, stats):
        j = 2 * jj
        scores(j + 1, sb_ref)
        stats = consume(j, sa_ref, stats, False)
        scores(j + 2, sa_ref)
        return consume(j + 1, sb_ref, stats, False)

    def even_tail(stats):
        return consume(last, sa_ref, stats, True)

    def odd_tail(stats):
        scores(last, sb_ref)
        stats = consume(last - 1, sa_ref, stats, False)
        return consume(last, sb_ref, stats, True)

    init = (jnp.full((1, tq), NEG_INF, F32), jnp.zeros((1, tq), F32))
    scores(0, sa_ref)
    stats = lax.fori_loop(0, last // 2, pair, (init, init))
    stats = lax.cond(last % 2 == 0, even_tail, odd_tail, stats)
    row = lax.broadcasted_iota(jnp.int32, (LANES, 1), 0)
    out = jnp.where(row < HEAD_DIM, acc_ref[0] / stats[0][1], acc_ref[1] / stats[1][1])
    o_ref[...] = out.T.astype(o_ref.dtype)


def _attn(qb, kaug, vt, *, batch, seq_len, tq, tk):
    n = qb.shape[0]
    nq = seq_len // tq
    kern = functools.partial(_attn_kernel, tq=tq, tk=tk)
    return pl.pallas_call(
        kern, grid=(batch, N_HEADS // 2, nq),
        in_specs=[
            pl.BlockSpec((tq, LANES), lambda b, p, i: (b * nq + i, p)),
            pl.BlockSpec((seq_len, 2 * LANES), lambda b, p, i: (b, p)),
            pl.BlockSpec((LANES, seq_len), lambda b, p, i: (p, b)),
        ],
        out_specs=pl.BlockSpec((tq, LANES), lambda b, p, i: (b * nq + i, p)),
        out_shape=jax.ShapeDtypeStruct((n, ATTN_W), BF16),
        scratch_shapes=[pltpu.VMEM((2, LANES, tq), F32), pltpu.VMEM((2, tk, tq), F32), pltpu.VMEM((2, tk, tq), F32)],
        compiler_params=_cparams(("arbitrary", "arbitrary", "arbitrary")), name="attn",
    )(qb, kaug, vt)


def _decode_query(q_ref):
    sub = lax.broadcasted_iota(jnp.int32, (N_HEADS, ATTN_W), 0)
    lane = lax.broadcasted_iota(jnp.int32, (N_HEADS, ATTN_W), 1)
    headmask = (lane // HEAD_DIM) == sub
    qbd = jnp.where(headmask, jnp.broadcast_to(q_ref[...].astype(F32), (N_HEADS, ATTN_W)), 0.0)
    return qbd.astype(BF16), headmask


def _decode_init(lfn_ref, m_ref, l_ref, acc_ref, carry_ref):
    m_ref[...] = jnp.full_like(m_ref, NEG_INF)
    l_ref[...] = jnp.zeros_like(l_ref)
    acc_ref[...] = jnp.zeros_like(acc_ref)
    carry_ref[...] = jnp.broadcast_to(lfn_ref[...], carry_ref.shape)


def _decode_pages(q_ref, su_ref, k_refs, v_refs, lf_refs, m_ref, l_ref, acc_ref, carry_ref):
    pp = len(k_refs)
    page = lf_refs[0].shape[1]
    qbd, _ = _decode_query(q_ref)
    parts = []
    for r in lf_refs:
        parts.extend(_split3(r[...]))
    sums = jnp.dot(jnp.concatenate(parts, axis=0), su_ref[...], preferred_element_type=F32)
    carry = carry_ref[...]
    ss = []
    for i in range(pp):
        base = 3 * N_HEADS * i
        both = sums[base:base + 8] + sums[base + 8:base + 16] + sums[base + 16:base + 24]
        s = jnp.dot(qbd, k_refs[i][...].astype(BF16), preferred_element_type=F32)
        ss.append(s + both[:, :page] + carry)
        carry = carry + both[:, page:]
    carry_ref[...] = carry
    m_old = m_ref[...]
    m = m_old
    for s in ss:
        m = jnp.maximum(m, s)
    alpha = jnp.exp(m_old - m)
    prs = [jnp.exp(s - m) for s in ss]
    l = alpha * l_ref[...]
    for pr in prs:
        l = l + pr
    for h in range(N_HEADS):
        rows = slice(h * HEAD_DIM, (h + 1) * HEAD_DIM)
        acc = acc_ref[rows, :] * alpha[h:h + 1, :]
        for i in range(pp):
            acc = acc + v_refs[i][rows, :] * prs[i][h:h + 1, :]
        acc_ref[rows, :] = acc
    m_ref[...] = m
    l_ref[...] = l


def _decode_finish(q_ref, kn_ref, vn_ref, o_ref, m_ref, l_ref, acc_ref):
    page = m_ref.shape[1]
    qbd, headmask = _decode_query(q_ref)
    m = m_ref[...]
    s_self = jnp.sum(qbd.astype(F32) * kn_ref[...].astype(F32), axis=-1, keepdims=True)
    m_all = jnp.maximum(jnp.max(m, axis=-1, keepdims=True), s_self)
    w = jnp.exp(m - m_all)
    w_self = jnp.exp(s_self - m_all)
    denom = jnp.sum(l_ref[...] * w, axis=-1, keepdims=True) + w_self
    ones = jnp.ones((N_HEADS, page), BF16)
    o = jnp.zeros((N_HEADS, ATTN_W), F32)
    weighted = jnp.concatenate(
        [acc_ref[h * HEAD_DIM:(h + 1) * HEAD_DIM, :] * w[h:h + 1, :] for h in range(N_HEADS)], axis=0)
    for part in _split3(weighted):
        o = o + lax.dot_general(ones, part, (((1,), (1,)), ((), ())), preferred_element_type=F32)
    o = o + w_self * jnp.broadcast_to(vn_ref[...].astype(F32), (N_HEADS, ATTN_W))
    o = jnp.where(headmask, o / denom, 0.0)
    o_ref[...] = jnp.sum(o, axis=0, keepdims=True).astype(o_ref.dtype)


class _DecodeWork:
    def __init__(self, page_table, qb, kb, vb, lf_new, cache_kt, cache_vt, cache_lf_t, pages_per_step):
        self.db, self.n_pages = page_table.shape
        self.pp = pages_per_step
        self.steps_per_seq = self.n_pages // self.pp
        self.n_steps = self.db * self.steps_per_seq
        page = cache_lf_t.shape[2]
        newer = (lax.broadcasted_iota(jnp.int32, (page, page), 0) >
                 lax.broadcasted_iota(jnp.int32, (page, page), 1))
        su = jnp.concatenate([newer.astype(BF16), jnp.ones((page, page), BF16)], axis=1)
        self.page_table = page_table
        self.operands = (qb, kb, vb, lf_new, su, cache_kt, cache_vt, cache_lf_t)
        self.page = page
        self.out_shape = jax.ShapeDtypeStruct((self.db, 1, ATTN_W), BF16)
        self.scratch = [pltpu.VMEM((N_HEADS, page), F32), pltpu.VMEM((N_HEADS, page), F32),
                        pltpu.VMEM((ATTN_W, page), F32), pltpu.VMEM((N_HEADS, page), F32),
                        pltpu.VMEM((2, self.pp, ATTN_W, page), F32), pltpu.VMEM((2, self.pp, ATTN_W, page), F32),
                        pltpu.VMEM((2, self.pp, N_HEADS, page), F32), pltpu.SemaphoreType.DMA((2, 3))]

    def specs(self, linear_step):
        page, sps = self.page, self.steps_per_seq

        def per_seq(shape):
            seq = lambda ids: jnp.minimum(linear_step(*ids), self.n_steps - 1) // sps
            return pl.BlockSpec((None,) + shape, lambda *a: (seq(a[:-1]), 0, 0))

        in_specs = [per_seq((1, ATTN_W)), per_seq((1, ATTN_W)), per_seq((1, ATTN_W)), per_seq((N_HEADS, 1)),
                    pl.BlockSpec((page, 2 * page), lambda *a: (0, 0))]
        in_specs += [pl.BlockSpec(memory_space=pl.ANY)] * 3
        return in_specs, per_seq((1, ATTN_W))

    def copies(self, pt_ref, step, slot, kt_hbm, vt_hbm, lf_hbm, kbuf, vbuf, lfbuf, sem):
        b = step // self.steps_per_seq
        j = step % self.steps_per_seq
        out = []
        for i in range(self.pp):
            pg = pt_ref[b, self.n_pages - 1 - (j * self.pp + i)]
            out.append(pltpu.make_async_copy(kt_hbm.at[pg], kbuf.at[slot, i], sem.at[slot, 0]))
            out.append(pltpu.make_async_copy(vt_hbm.at[pg], vbuf.at[slot, i], sem.at[slot, 1]))
            out.append(pltpu.make_async_copy(lf_hbm.at[pg], lfbuf.at[slot, i], sem.at[slot, 2]))
        return out


CONV_HALO = 32
CONV_ROWS = 32


def _ln_silu(y, g, b):
    mu = jnp.mean(y, axis=-1, keepdims=True)
    d = y - mu
    var = jnp.mean(d * d, axis=-1, keepdims=True)
    z = d * lax.rsqrt(var + EPS) * g + b
    return z * jax.nn.sigmoid(z)


def _conv_kernel(cur_ref, prev_ref, w_ref, b_ref, g_ref, lb_ref, o_ref, xs_ref, *, tiles_per_seq):
    t = cur_ref.shape[0]
    i = pl.program_id(0)
    xs_ref[0, 0:CONV_HALO, :] = jnp.where(i % tiles_per_seq == 0, 0.0, prev_ref[...])
    xs_ref[0, CONV_HALO:, :] = cur_ref[...]
    off = CONV_HALO - (CONV_W - 1)
    span = t + CONV_HALO - SUBLANES
    for r in range(1, SUBLANES):
        xs_ref[r, 0:span, :] = xs_ref[0, pl.ds(r, span), :]
    for c in range(t // CONV_ROWS):
        acc = jnp.broadcast_to(b_ref[...], (CONV_ROWS, CONV_CH))
        for w in range(CONV_W):
            r, base = (off + w) % SUBLANES, (off + w) // SUBLANES * SUBLANES
            acc = acc + xs_ref[r, pl.ds(c * CONV_ROWS + base, CONV_ROWS), :] * w_ref[w:w + 1, :]
        o_ref[c * CONV_ROWS:(c + 1) * CONV_ROWS, :] = _ln_silu(acc, g_ref[...], lb_ref[...]).astype(o_ref.dtype)


def _conv_prompt(u2d, conv_w, conv_b, ln_g, ln_b, *, tile, seq_len):
    n = u2d.shape[0]
    ratio = tile // CONV_HALO
    kern = functools.partial(_conv_kernel, tiles_per_seq=seq_len // tile)
    wts = (conv_w, conv_b, ln_g, ln_b)
    return pl.pallas_call(
        kern, grid=(n // tile,),
        in_specs=[pl.BlockSpec((tile, CONV_CH), lambda i: (i, 0)),
                  pl.BlockSpec((CONV_HALO, CONV_CH), lambda i: (jnp.maximum(i * ratio - 1, 0), 0))]
                 + [_full_spec(w.shape) for w in wts],
        out_specs=pl.BlockSpec((tile, CONV_CH), lambda i: (i, 0)),
        out_shape=jax.ShapeDtypeStruct((n, CONV_CH), BF16),
        scratch_shapes=[pltpu.VMEM((SUBLANES, tile + CONV_HALO, CONV_CH), F32)],
        compiler_params=_cparams(("arbitrary",)), name="conv_prompt",
    )(u2d, u2d, *wts)


def _conv_step_kernel(xp_ref, w_ref, b_ref, g_ref, lb_ref, o_ref):
    acc = jnp.broadcast_to(b_ref[...], o_ref.shape)
    for w in range(CONV_W):
        acc = acc + xp_ref[w] * w_ref[w:w + 1, :]
    o_ref[...] = _ln_silu(acc, g_ref[...], lb_ref[...]).astype(o_ref.dtype)


def _conv_step(xp_t, conv_w, conv_b, ln_g, ln_b):
    db = xp_t.shape[1]
    args = (xp_t, conv_w, conv_b, ln_g, ln_b)
    return pl.pallas_call(
        _conv_step_kernel, grid=(1,),
        in_specs=[_full_spec(a.shape) for a in args],
        out_specs=_full_spec((db, CONV_CH)),
        out_shape=jax.ShapeDtypeStruct((db, CONV_CH), BF16),
        compiler_params=_cparams(("arbitrary",)), name="conv_step",
    )(*args)


GATE_PITCH = PEER_KEYS + SUBLANES


def _gate_grid(tok, sub, i1_ref, i2_ref, g_ref, stage_ref):
    i1 = i1_ref[pl.ds(tok, 1), :]
    i2 = i2_ref[pl.ds(tok, 1), :]
    g = g_ref[pl.ds(tok, 1), :]
    p1 = jnp.where(sub == i1, g, 0.0).astype(BF16)
    p2 = jnp.where(sub == i2, 1.0, 0.0).astype(BF16)
    grid = lax.dot_general(p1, p2, (((1,), (1,)), ((), ())), preferred_element_type=F32)
    stage_ref[pl.ds(pl.multiple_of(tok * GATE_PITCH, SUBLANES), PEER_KEYS), :] = grid


def _gate_rows(a, t, stage_ref, o_ref):
    rows = stage_ref[pl.ds(a, t, stride=GATE_PITCH), :]
    o_ref[:, pl.ds(pl.multiple_of(a * PEER_KEYS, PEER_KEYS), PEER_KEYS)] = rows.astype(o_ref.dtype)


N_CAND = 64
ROUTE_CHAINS = 4


def _top16(chains, iota_f):
    ss = [c[0] for c in chains]
    for r in range(PEER_TOPK):
        for n, (_, val_ref, idx_ref) in enumerate(chains):
            m = jnp.max(ss[n], axis=0, keepdims=True)
            idx = jnp.min(jnp.where(ss[n] == m, iota_f, float(1 << 20)), axis=0, keepdims=True)
            val_ref[r:r + 1, :] = m
            idx_ref[r:r + 1, :] = idx
            ss[n] = jnp.where(iota_f == idx, NEG_INF, ss[n])


def _batcher_network(n):
    pairs = []
    p = 1
    while p < n:
        k = p
        while k >= 1:
            for j in range(k % p, n - k, 2 * k):
                for i in range(min(k, n - j - k)):
                    if (i + j) // (2 * p) == (i + j + k) // (2 * p):
                        pairs.append((i + j, i + j + k))
            k //= 2
        p *= 2
    return pairs


def _top16_columns(srcs, outs):
    t = srcs[0].shape[1]
    groups = PEER_KEYS // SUBLANES
    sub = lax.broadcasted_iota(jnp.int32, (SUBLANES, t), 0).astype(F32)
    vals = [[s[SUBLANES * j:SUBLANES * (j + 1), :] for j in range(groups)] for s in srcs]
    idxs = [[sub + float(SUBLANES * j) for j in range(groups)] for _ in srcs]
    for a, b in _batcher_network(groups):
        for v, ix in zip(vals, idxs):
            swap = v[b] > v[a]
            v[a], v[b] = jnp.maximum(v[a], v[b]), jnp.minimum(v[a], v[b])
            ix[a], ix[b] = jnp.where(swap, ix[b], ix[a]), jnp.where(swap, ix[a], ix[b])
    flag = jnp.zeros((SUBLANES, t), F32)
    for v in vals:
        for d in range(groups - 1):
            flag = jnp.maximum(flag, jnp.where(v[d] == v[d + 1], 1.0, 0.0))
    for r in range(PEER_TOPK):
        for v, ix, (val_ref, idx_ref) in zip(vals, idxs, outs):
            m = jnp.max(v[0], axis=0, keepdims=True)
            idx = jnp.min(jnp.where(v[0] == m, ix[0], float(1 << 20)), axis=0, keepdims=True)
            val_ref[r:r + 1, :] = m
            idx_ref[r:r + 1, :] = idx
            won = ix[0] == idx
            for d in range(PEER_TOPK - 1 - r):
                v[d] = jnp.where(won, v[d + 1], v[d])
                ix[d] = jnp.where(won, ix[d + 1], ix[d])
    return flag


def _candidate_ids(t):
    row = lax.broadcasted_iota(jnp.int32, (N_CAND, t), 0)
    r1 = jnp.where(row < 16, 0, jnp.where(row < 40, (row - 8) // 8, jnp.where(row < 56, (row - 24) // 4, row - 48)))
    r2 = jnp.where(row < 16, row, jnp.where(row < 40, row % 8, jnp.where(row < 56, row % 4, 0)))
    return (r1 * PEER_TOPK + r2).astype(F32)


def _route_kernel(x_ref, a_ref, c_ref, woa_ref, woc_ref, g2_ref, wq_ref, sk_ref,
                  y_ref, h_ref, gd_ref,
                  sc_ref, tv_ref, ti_ref, cand_ref, bs_ref, o1_ref, o2_ref, og_ref, p1_ref, p2_ref, pg_ref, stage_ref):
    t = x_ref.shape[0]

    @pl.when(pl.program_id(0) == 0)
    def _():
        p1_ref[...] = jnp.zeros_like(p1_ref)
        p2_ref[...] = jnp.zeros_like(p2_ref)
        pg_ref[...] = jnp.zeros_like(pg_ref)

    y = (x_ref[...] + jnp.dot(a_ref[...], woa_ref[...], preferred_element_type=F32)
         + jnp.dot(c_ref[...], woc_ref[...], preferred_element_type=F32))
    y_ref[...] = y
    ms = jnp.mean(y * y, axis=-1, keepdims=True)
    h = (y * lax.rsqrt(ms + EPS) * g2_ref[...]).astype(BF16)
    h_ref[...] = h
    q = jnp.dot(h, wq_ref[...], preferred_element_type=F32).astype(BF16)
    for hp in range(2 * PEER_HEADS):
        sc_ref[hp] = lax.dot_general(sk_ref[hp], q[:, hp * PEER_KEYS:(hp + 1) * PEER_KEYS],
                                     (((1,), (1,)), ((), ())), preferred_element_type=F32)

    key_iota = lax.broadcasted_iota(jnp.int32, (PEER_KEYS, t), 0).astype(F32)

    pick_sub = lax.broadcasted_iota(jnp.int32, (PEER_KEYS, PEER_HEADS * PEER_TOPK), 0).astype(F32)
    tokens_per_trip = t // (2 * PEER_HEADS // ROUTE_CHAINS)

    def stage1(it, c):
        hps = [it * ROUTE_CHAINS + n for n in range(ROUTE_CHAINS)]
        outs = [(tv_ref.at[hp], ti_ref.at[hp]) for hp in hps]
        unordered_ties = _top16_columns([sc_ref[hp] for hp in hps], outs)
        for u in range(tokens_per_trip):
            _gate_grid(it * tokens_per_trip + u, pick_sub, p1_ref, p2_ref, pg_ref, stage_ref)

        @pl.when(jnp.max(unordered_ties) > 0.0)
        def _():
            _top16([(sc_ref[hp], val_ref, idx_ref) for hp, (val_ref, idx_ref) in zip(hps, outs)], key_iota)

        return c

    lax.fori_loop(0, 2 * PEER_HEADS // ROUTE_CHAINS, stage1, 0)

    cand_id = _candidate_ids(t)
    rank_iota = lax.broadcasted_iota(jnp.int32, (PEER_TOPK, t), 0).astype(F32)
    low4 = lax.broadcasted_iota(jnp.int32, (SUBLANES, t), 0) < 4

    def stage2(it, c):
        heads = [it * ROUTE_CHAINS + n for n in range(ROUTE_CHAINS)]
        ss, i1s, i2s = [], [], []
        for n, hd in enumerate(heads):
            v1 = tv_ref[2 * hd]
            v2 = tv_ref[2 * hd + 1]
            i1s.append(ti_ref[2 * hd])
            i2s.append(ti_ref[2 * hd + 1])
            v2lo = v2[0:SUBLANES, :]
            v2q = jnp.where(low4, v2lo, pltpu.roll(v2lo, 4, axis=0))
            cand_ref[n, 0:16, :] = v1[0:1, :] + v2
            for r1 in (1, 2, 3):
                cand_ref[n, 8 + 8 * r1:16 + 8 * r1, :] = v1[r1:r1 + 1, :] + v2lo
            cand_ref[n, 40:48, :] = jnp.where(low4, v1[4:5, :], v1[5:6, :]) + v2q
            cand_ref[n, 48:56, :] = jnp.where(low4, v1[6:7, :], v1[7:8, :]) + v2q
            cand_ref[n, 56:64, :] = v1[8:16, :] + v2[0:1, :]
            ss.append(cand_ref[n])
        for r in range(PEER_TOPK):
            for n, hd in enumerate(heads):
                base = pl.multiple_of(hd * PEER_TOPK, PEER_TOPK)
                m = jnp.max(ss[n], axis=0, keepdims=True)
                cid = jnp.min(jnp.where(ss[n] == m, cand_id, float(1 << 20)), axis=0, keepdims=True)
                ss[n] = jnp.where(cand_id == cid, NEG_INF, ss[n])
                r1 = jnp.floor(cid * (1.0 / PEER_TOPK))
                r2 = cid - r1 * PEER_TOPK
                bs_ref[n, r:r + 1, :] = m
                o1_ref[pl.ds(base + r, 1), :] = jnp.max(jnp.where(rank_iota == r1, i1s[n], -1.0), axis=0, keepdims=True)
                o2_ref[pl.ds(base + r, 1), :] = jnp.max(jnp.where(rank_iota == r2, i2s[n], -1.0), axis=0, keepdims=True)
        for n, hd in enumerate(heads):
            base = pl.multiple_of(hd * PEER_TOPK, PEER_TOPK)
            bs = bs_ref[n]
            e = jnp.exp(bs - bs[0:1, :])
            og_ref[pl.ds(base, PEER_TOPK), :] = e / jnp.sum(e, axis=0, keepdims=True)
        rows_per_trip = PEER_KEYS // (PEER_HEADS // ROUTE_CHAINS)
        for u in range(rows_per_trip):
            _gate_rows(it * rows_per_trip + u, t, stage_ref, gd_ref)
        return c

    lax.fori_loop(0, PEER_HEADS // ROUTE_CHAINS, stage2, 0)
    p1_ref[...] = o1_ref[...].T
    p2_ref[...] = o2_ref[...].T
    pg_ref[...] = og_ref[...].T


def _route(x2d, a2d, c2d, wts, *, tile):
    n = x2d.shape[0]
    nt = n // tile
    hk = PEER_HEADS * PEER_TOPK
    row = lambda w: pl.BlockSpec((tile, w), lambda i: (jnp.minimum(i, nt - 1), 0))
    out_shape = (jax.ShapeDtypeStruct((n, D_MODEL), F32),
                 jax.ShapeDtypeStruct((n, D_MODEL), BF16),
                 jax.ShapeDtypeStruct((n, N_EXPERTS), BF16))
    return pl.pallas_call(
        _route_kernel, grid=(nt + 1,),
        in_specs=[row(D_MODEL), row(ATTN_W), row(CONV_CH)] + [_full_spec(w.shape) for w in wts],
        out_specs=(row(D_MODEL), row(D_MODEL),
                   pl.BlockSpec((tile, N_EXPERTS), lambda i: (jnp.maximum(i - 1, 0), 0))),
        out_shape=out_shape,
        scratch_shapes=[pltpu.VMEM((2 * PEER_HEADS, PEER_KEYS, tile), F32),
                        pltpu.VMEM((2 * PEER_HEADS, PEER_TOPK, tile), F32),
                        pltpu.VMEM((2 * PEER_HEADS, PEER_TOPK, tile), F32),
                        pltpu.VMEM((ROUTE_CHAINS, N_CAND, tile), F32),
                        pltpu.VMEM((ROUTE_CHAINS, PEER_TOPK, tile), F32),
                        pltpu.VMEM((hk, tile), F32), pltpu.VMEM((hk, tile), F32), pltpu.VMEM((hk, tile), F32),
                        pltpu.VMEM((tile, hk), F32), pltpu.VMEM((tile, hk), F32), pltpu.VMEM((tile, hk), F32),
                        pltpu.VMEM((tile * GATE_PITCH, PEER_KEYS), F32)],
        compiler_params=_cparams(("arbitrary",)), name="route",
    )(x2d, a2d, c2d, *wts)


def _experts_kernel(pt_ref, h_ref, g_ref, u_ref, v_ref, y_ref, *rest, decode):
    if decode is None:
        o_ref, acc_ref, even_ref, odd_ref = rest
    else:
        q_ref, kn_ref, vn_ref, lfn_ref, su_ref, kt_hbm, vt_hbm, lf_hbm = rest[:8]
        (o_ref, od_ref, acc_ref, even_ref, odd_ref, m_ref, l_ref, dacc_ref, carry_ref,
         kbuf, vbuf, lfbuf, sem) = rest[8:]
        state = (m_ref, l_ref, dacc_ref, carry_ref)
        n_host = pl.num_programs(0) * pl.num_programs(1)
        step = pl.program_id(0) * pl.num_programs(1) + pl.program_id(1)
        active = step < decode.n_steps
        cur = jnp.minimum(step, decode.n_steps - 1)
        nxt = jnp.minimum(step + 1, decode.n_steps - 1)
        seq_step = cur % decode.steps_per_seq
        slot = step % 2
        copies = functools.partial(decode.copies, pt_ref, kt_hbm=kt_hbm, vt_hbm=vt_hbm, lf_hbm=lf_hbm,
                                   kbuf=kbuf, vbuf=vbuf, lfbuf=lfbuf, sem=sem)
    e = pl.program_id(1)

    @pl.when(e == 0)
    def _():
        acc_ref[...] = jnp.zeros_like(acc_ref)
        odd_ref[...] = jnp.zeros_like(odd_ref)

    if decode is not None:
        pl.when(active & (seq_step == 0))(lambda: _decode_init(lfn_ref, *state))

        @pl.when(step == 0)
        def _():
            for c in copies(0, 0):
                c.start()

    def work(cur_ref, prev_ref):
        if decode is not None:
            for c in copies(nxt, 1 - slot):
                c.start()
            for c in copies(cur, slot):
                c.wait()
            pages = lambda buf: [buf.at[slot, i] for i in range(decode.pp)]
            _decode_pages(q_ref, su_ref, pages(kbuf), pages(vbuf), pages(lfbuf), *state)
        h = h_ref[...]
        for c in range(u_ref.shape[0] // MXU_TILE):
            cols = slice(c * MXU_TILE, (c + 1) * MXU_TILE)
            s = lax.dot_general(h, u_ref[cols, :], (((1,), (1,)), ((), ())), preferred_element_type=F32)
            act = 0.5 * s * (1.0 + lax.erf(s * math.sqrt(0.5)))
            cur_ref[:, cols] = (g_ref[:, cols].astype(F32) * act).astype(BF16)
        prev = prev_ref[...]
        for c in range(v_ref.shape[1] // MXU_TILE):
            cols = slice(c * MXU_TILE, (c + 1) * MXU_TILE)
            acc_ref[:, cols] += jnp.dot(prev, v_ref[:, cols], preferred_element_type=F32)

    pl.when(e % 2 == 0)(lambda: work(even_ref, odd_ref))
    pl.when(e % 2 == 1)(lambda: work(odd_ref, even_ref))

    if decode is not None:
        pl.when(active & (seq_step == decode.steps_per_seq - 1))(
            lambda: _decode_finish(q_ref, kn_ref, vn_ref, od_ref, m_ref, l_ref, dacc_ref))

        @pl.when(step == n_host - 1)
        def _():
            for c in copies(nxt, 1 - slot):
                c.wait()

    @pl.when(e == pl.num_programs(1) - 1)
    def _():
        o_ref[...] = y_ref[...] + acc_ref[...]


def _experts(h2d, gdense, u_tab, v_tab, y2d, *, tile, eblk, decode=None):
    n = h2d.shape[0]
    nb = N_EXPERTS // eblk
    grid = (n // tile, nb + 1)
    scored = lambda e: jnp.minimum(e, nb - 1)
    folded = lambda e: jnp.maximum(e - 1, 0)
    in_specs = [pl.BlockSpec((tile, D_MODEL), lambda i, e, pt: (i, 0)),
                pl.BlockSpec((tile, eblk), lambda i, e, pt: (i, scored(e))),
                pl.BlockSpec((eblk, D_MODEL), lambda i, e, pt: (scored(e), 0)),
                pl.BlockSpec((eblk, D_MODEL), lambda i, e, pt: (folded(e), 0)),
                pl.BlockSpec((tile, D_MODEL), lambda i, e, pt: (i, 0))]
    out_specs = [pl.BlockSpec((tile, D_MODEL), lambda i, e, pt: (i, 0))]
    out_shape = [jax.ShapeDtypeStruct((n, D_MODEL), F32)]
    scratch = [pltpu.VMEM((tile, D_MODEL), F32), pltpu.VMEM((tile, eblk), BF16), pltpu.VMEM((tile, eblk), BF16)]
    operands = [h2d, gdense, u_tab, v_tab, y2d]
    if decode is None:
        prefetch = jnp.zeros((1, 1), jnp.int32)
    else:
        assert decode.n_steps <= grid[0] * grid[1], "not enough host steps for the paged attention"
        prefetch = decode.page_table
        dec_in, dec_out = decode.specs(lambda i, e: i * grid[1] + e)
        in_specs += dec_in
        out_specs.append(dec_out)
        out_shape.append(decode.out_shape)
        scratch += decode.scratch
        operands += list(decode.operands)
    grid_spec = pltpu.PrefetchScalarGridSpec(
        num_scalar_prefetch=1, grid=grid, in_specs=in_specs, out_specs=out_specs, scratch_shapes=scratch)
    outs = pl.pallas_call(
        functools.partial(_experts_kernel, decode=decode), grid_spec=grid_spec, out_shape=out_shape,
        compiler_params=_cparams(("arbitrary", "arbitrary")), name="experts",
    )(prefetch, *operands)
    return outs[0] if decode is None else tuple(outs)


PROJ_TILE = 512
ATTN_TILE = 512
ATTN_KEY_TILE = 512
CONV_TILE = 256
ROUTE_TILE = 128
EXPERT_TILE = 1024
EXPERT_BLOCK = 512
DECODE_PAGES_PER_STEP = 8


def _finish(x2d, a2d, c2d, route_wts, u_tab, v_tab, *, expert_tile, decode=None):
    n = x2d.shape[0]
    y, h2, gd = _route(x2d, a2d, c2d, route_wts, tile=min(ROUTE_TILE, n))
    return _experts(h2, gd, u_tab, v_tab, y, tile=expert_tile, eblk=EXPERT_BLOCK, decode=decode)


def kernel(x_prompt, x_sample, cache_k, cache_v, cache_logf, state_conv, page_table, norm1_g, w_in, b_forget,
           q_gain, k_gain, conv_w, conv_b, conv_ln_g, conv_ln_b, w_out, norm2_g, peer_wq, peer_subkeys,
           peer_u, peer_v):
    assert w_in.shape[0] == 1, "single layer"
    batch, seq_len, _ = x_prompt.shape
    db = x_sample.shape[0]
    n_pool, page = cache_k.shape[1], cache_k.shape[2]

    w = w_in[0].astype(BF16)
    w_qkv = w[:, :3 * ATTN_W]
    w_ag = w[:, 3 * ATTN_W:3 * ATTN_W + 2 * CONV_CH]
    w_fg = jnp.pad(w[:, 3 * ATTN_W + 2 * CONV_CH:], ((0, 0), (0, LANES - N_HEADS)))
    b_fg = jnp.pad(b_forget[0].reshape(1, N_HEADS), ((0, 0), (0, LANES - N_HEADS)))
    qg = jnp.tile(q_gain[0], N_HEADS).reshape(1, ATTN_W)
    kg = jnp.tile(k_gain[0], N_HEADS).reshape(1, ATTN_W)
    gi = lax.broadcasted_iota(jnp.int32, (ATTN_W, ATTN_W), 0) // HEAD_DIM
    gj = lax.broadcasted_iota(jnp.int32, (ATTN_W, ATTN_W), 1) // HEAD_DIM
    bd = (gi == gj).astype(BF16)
    tri = (lax.broadcasted_iota(jnp.int32, (LANES, LANES), 1) <=
           lax.broadcasted_iota(jnp.int32, (LANES, LANES), 0)).astype(BF16)
    src = lax.broadcasted_iota(jnp.int32, (ATTN_W, N_HEADS * LANES), 0)
    dst = lax.broadcasted_iota(jnp.int32, (ATTN_W, N_HEADS * LANES), 1)
    place_k = (dst == (src // HEAD_DIM) * LANES + src % HEAD_DIM).astype(BF16)
    src = lax.broadcasted_iota(jnp.int32, (N_BIAS_TERMS * LANES, N_HEADS * LANES), 0)
    dst = lax.broadcasted_iota(jnp.int32, (N_BIAS_TERMS * LANES, N_HEADS * LANES), 1)
    place_c = ((src % LANES < N_HEADS) & (dst == (src % LANES) * LANES + HEAD_DIM + src // LANES)).astype(BF16)
    proj_wts = (norm1_g[0].reshape(1, D_MODEL), w_qkv, w_ag, w_fg, b_fg, qg, kg, bd, tri, place_k, place_c)
    conv_wts = (conv_w[0], conv_b[0].reshape(1, CONV_CH), conv_ln_g[0].reshape(1, CONV_CH),
                conv_ln_b[0].reshape(1, CONV_CH))
    wo = w_out[0].astype(BF16)
    sk = peer_subkeys[0].astype(BF16).reshape(2 * PEER_HEADS, PEER_KEYS, PEER_KEYS)
    route_wts = (wo[:ATTN_W], wo[ATTN_W:], norm2_g[0].reshape(1, D_MODEL), peer_wq[0].astype(BF16), sk)
    u_tab = peer_u[0].astype(BF16)
    v_tab = peer_v[0].astype(BF16)

    xp2 = x_prompt.reshape(batch * seq_len, D_MODEL)
    k_p, v_p, u_p, lf_p, qb, _, _, kaug, vt = _proj(
        xp2, proj_wts, tile=PROJ_TILE, seq_len=seq_len, with_cumsum=True)
    xs2 = x_sample.reshape(db, D_MODEL)
    k_s, v_s, u_s, lf_s, qsb, ksb, vsb, _, _ = _proj(xs2, proj_wts, tile=db, seq_len=db, with_cumsum=False)

    lf_t = cache_logf[0].transpose(0, 2, 1)
    row = lambda a: a.reshape(db, 1, ATTN_W)
    transposed = lambda c: c[0].transpose(0, 2, 3, 1).reshape(n_pool, ATTN_W, page)
    decode = _DecodeWork(page_table, row(qsb), row(ksb), row(vsb), lf_s.reshape(db, N_HEADS, 1),
                         transposed(cache_k), transposed(cache_v), lf_t, DECODE_PAGES_PER_STEP)

    a_p = _attn(qb, kaug, vt, batch=batch, seq_len=seq_len, tq=ATTN_TILE, tk=ATTN_KEY_TILE)
    c_p = _conv_prompt(u_p, *conv_wts, tile=CONV_TILE, seq_len=seq_len)
    y_p, a_s = _finish(xp2, a_p, c_p, route_wts, u_tab, v_tab, expert_tile=EXPERT_TILE, decode=decode)

    window = jnp.concatenate([state_conv[0], u_s[:, None, :]], axis=1)
    c_s = _conv_step(window.transpose(1, 0, 2), *conv_wts)
    y_s = _finish(xs2, a_s.reshape(db, ATTN_W), c_s, route_wts, u_tab, v_tab, expert_tile=db)

    tail = CONV_W - 1
    state = lambda t, seqs, toks: t.reshape(seqs, N_HEADS, HEAD_DIM, toks).transpose(0, 3, 1, 2)[None]
    return (y_p.reshape(batch, seq_len, D_MODEL),
            y_s.reshape(db, 1, D_MODEL),
            state(k_p, batch, seq_len),
            state(v_p, batch, seq_len),
            lf_p.reshape(1, batch, seq_len, N_HEADS),
            u_p.reshape(batch, seq_len, CONV_CH)[:, seq_len - tail:][None],
            state(k_s, 1, db).reshape(1, db, 1, N_HEADS, HEAD_DIM),
            state(v_s, 1, db).reshape(1, db, 1, N_HEADS, HEAD_DIM),
            lf_s.reshape(1, db, 1, N_HEADS),
            window[:, 1:][None])
```

```python
import functools
import math

import jax
import jax.numpy as jnp
from jax import lax
from jax.experimental import pallas as pl
from jax.experimental.pallas import tpu as pltpu

F32 = jnp.float32
BF16 = jnp.bfloat16

D_MODEL = 1024
HEAD_DIM = 64
N_HEADS = 8
ATTN_W = N_HEADS * HEAD_DIM
CONV_CH = D_MODEL - ATTN_W
CONV_W = 31
PEER_HEADS = 8
PEER_KEYS = 128
PEER_TOPK = 16
N_EXPERTS = PEER_KEYS * PEER_KEYS
EPS = 1e-6
LANES = 128
SUBLANES = 8
MXU_TILE = 256
VMEM_LIMIT = 56 * 1024 * 1024
NEG_INF = float("-inf")


def _cparams(sem):
    return pltpu.CompilerParams(dimension_semantics=sem, vmem_limit_bytes=VMEM_LIMIT)


def _split3(x):
    p1 = x.astype(BF16)
    r1 = x - p1.astype(F32)
    p2 = r1.astype(BF16)
    p3 = (r1 - p2.astype(F32)).astype(BF16)
    return p1, p2, p3


def _full_spec(shape):
    return pl.BlockSpec(shape, lambda *_: (0,) * len(shape))


def _group_rms(t, bd, gain):
    sq = t * t
    hi = sq.astype(BF16)
    lo = (sq - hi.astype(F32)).astype(BF16)
    ss = jnp.dot(hi, bd, preferred_element_type=F32) + jnp.dot(lo, bd, preferred_element_type=F32)
    return t * lax.rsqrt(ss * (1.0 / HEAD_DIM) + EPS) * gain


def _proj_kernel(x_ref, g1_ref, wqkv_ref, wag_ref, wfg_ref, bf_ref, qg_ref, kg_ref, bd_ref, tri_ref, pk_ref, pc_ref,
                 cw_ref, cb_ref, cg_ref, clb_ref,
                 k_ref, v_ref, u_ref, lf_ref, qb_ref, kb_ref, vb_ref, kaug_ref, vt_ref, c_ref,
                 carry_ref, xs_ref, *, tiles_per_seq, with_cumsum):
    t = x_ref.shape[0]
    x = x_ref[...]
    ms = jnp.mean(x * x, axis=-1, keepdims=True)
    h = (x * lax.rsqrt(ms + EPS) * g1_ref[...]).astype(BF16)

    qkv = jnp.dot(h, wqkv_ref[...], preferred_element_type=F32)
    bd = bd_ref[...]
    q = _group_rms(qkv[:, :ATTN_W], bd, qg_ref[...])
    k = _group_rms(qkv[:, ATTN_W:2 * ATTN_W], bd, kg_ref[...])
    v = qkv[:, 2 * ATTN_W:]
    v_t = v.T
    k_ref[...] = k.T
    v_ref[...] = v_t
    qb_ref[...] = (q * (HEAD_DIM ** -0.5)).astype(BF16)
    kb_ref[...] = k.astype(BF16)
    vb_ref[...] = v.astype(BF16)

    ag = jnp.dot(h, wag_ref[...], preferred_element_type=F32)
    u = ag[:, :CONV_CH] * jax.nn.sigmoid(ag[:, CONV_CH:])
    u_ref[...] = u

    z = jnp.dot(h, wfg_ref[...], preferred_element_type=F32) + bf_ref[...]
    logf = jnp.minimum(z, 0.0) - jnp.log1p(jnp.exp(-jnp.abs(z)))
    lf_ref[...] = logf[:, :N_HEADS]

    if with_cumsum:
        vt_ref[...] = v_t.astype(BF16)
        i = pl.program_id(0)
        _conv_tile(u, i % tiles_per_seq == 0, cw_ref, cb_ref, cg_ref, clb_ref, c_ref, xs_ref)
        lane = lax.broadcasted_iota(jnp.int32, (1, LANES), 1)
        carry = jnp.where(i % tiles_per_seq == 0, 0.0, carry_ref[...])
        tri = tri_ref[...]
        for sb in range(t // LANES):
            rows = slice(sb * LANES, (sb + 1) * LANES)
            blk = jnp.where(lane < N_HEADS, logf[rows], 0.0)
            p1, p2, p3 = _split3(blk)
            cs = (jnp.dot(tri, p1, preferred_element_type=F32)
                  + jnp.dot(tri, p2, preferred_element_type=F32)
                  + jnp.dot(tri, p3, preferred_element_type=F32)) + carry
            carry = cs[LANES - 1:LANES, :]
            placed = (jnp.dot(k[rows].astype(BF16), pk_ref[...], preferred_element_type=F32)
                      + jnp.dot(jnp.concatenate(_split3(-cs), axis=1), pc_ref[...], preferred_element_type=F32))
            kaug_ref[rows, :] = placed.astype(BF16)
        carry_ref[...] = carry
    else:
        kaug_ref[...] = jnp.zeros_like(kaug_ref)
        vt_ref[...] = jnp.zeros_like(vt_ref)
        c_ref[...] = jnp.zeros_like(c_ref)


def _proj(x2d, wts, *, tile, seq_len, with_cumsum):
    n = x2d.shape[0]
    nt = n // tile
    row = lambda w: pl.BlockSpec((tile, w), lambda i: (i, 0))
    tps = seq_len // tile
    kern = functools.partial(_proj_kernel, tiles_per_seq=tps, with_cumsum=with_cumsum)
    state_t = pl.BlockSpec((ATTN_W, tile), lambda i: (i // tps, i % tps))
    out_shape = (
        jax.ShapeDtypeStruct((n // seq_len * ATTN_W, seq_len), F32),
        jax.ShapeDtypeStruct((n // seq_len * ATTN_W, seq_len), F32),
        jax.ShapeDtypeStruct((n, CONV_CH), F32),
        jax.ShapeDtypeStruct((n, N_HEADS), F32),
        jax.ShapeDtypeStruct((n, ATTN_W), BF16),
        jax.ShapeDtypeStruct((n, ATTN_W), BF16),
        jax.ShapeDtypeStruct((n, ATTN_W), BF16),
        jax.ShapeDtypeStruct((n, N_HEADS * LANES), BF16),
        jax.ShapeDtypeStruct((ATTN_W, n), BF16),
        jax.ShapeDtypeStruct((n, CONV_CH), BF16),
    )
    out_specs = (state_t, state_t, row(CONV_CH), row(N_HEADS), row(ATTN_W), row(ATTN_W), row(ATTN_W),
                 row(N_HEADS * LANES), pl.BlockSpec((ATTN_W, tile), lambda i: (0, i)), row(CONV_CH))
    in_specs = [row(D_MODEL)] + [_full_spec(w.shape) for w in wts]
    return pl.pallas_call(
        kern, grid=(nt,), in_specs=in_specs, out_specs=out_specs, out_shape=out_shape,
        scratch_shapes=[pltpu.VMEM((1, LANES), F32),
                        pltpu.VMEM((SUBLANES, max(tile, CONV_HALO) + CONV_HALO, CONV_CH), F32)],
        compiler_params=_cparams(("arbitrary",)), name="proj",
    )(x2d, *wts)


N_BIAS_TERMS = 3


def _attn_kernel(q_ref, kaug_ref, vt_ref, o_ref, acc_ref, sa_ref, sb_ref, *, tq, tk):
    qi = pl.program_id(2)
    lane = lax.broadcasted_iota(jnp.int32, (1, LANES), 1)
    q = q_ref[...].astype(F32)
    ones = jnp.where(lane < HEAD_DIM + N_BIAS_TERMS, 1.0, 0.0)
    key_row = lax.broadcasted_iota(jnp.int32, (tk, tq), 0)
    qry_col = lax.broadcasted_iota(jnp.int32, (tk, tq), 1)
    last = (qi * tq) // tk
    visible = key_row + last * tk <= qry_col + qi * tq
    qhs = []
    for hh in range(2):
        qh = q if hh == 0 else pltpu.roll(q, HEAD_DIM, axis=1)
        qhs.append(jnp.where(lane < HEAD_DIM, qh, ones).astype(BF16))
    acc_ref[...] = jnp.zeros_like(acc_ref)

    def scores(j, s_ref):
        start = pl.multiple_of(j * tk, tk)
        for hh in range(2):
            s_ref[hh] = lax.dot_general(kaug_ref[pl.ds(start, tk), hh * LANES:(hh + 1) * LANES], qhs[hh],
                                        (((1,), (1,)), ((), ())), preferred_element_type=F32)

    def consume(j, s_ref, stats, masked):
        start = pl.multiple_of(j * tk, tk)
        vt = vt_ref[:, pl.ds(start, tk)]
        new_stats = []
        for hh in range(2):
            m, l = stats[hh]
            s = jnp.where(visible, s_ref[hh], NEG_INF) if masked else s_ref[hh]
            m_new = jnp.maximum(m, jnp.max(s, axis=0, keepdims=True))
            alpha = jnp.exp(m - m_new)
            pr = jnp.exp(s - m_new)
            l = alpha * l + jnp.sum(pr, axis=0, keepdims=True)
            acc_ref[hh] = alpha * acc_ref[hh] + jnp.dot(vt, pr.astype(BF16), preferred_element_type=F32)
            new_stats.append((m_new, l))
        return tuple(new_stats)

    def pair(jj, stats):
        j = 2 * jj
        scores(j + 1, sb_ref)
        stats = consume(j, sa_ref, stats, False)
        scores(j + 2, sa_ref)
        return consume(j + 1, sb_ref, stats, False)

    def even_tail(stats):
        return consume(last, sa_ref, stats, True)

    def odd_tail(stats):
        scores(last, sb_ref)
        stats = consume(last - 1, sa_ref, stats, False)
        return consume(last, sb_ref, stats, True)

    init = (jnp.full((1, tq), NEG_INF, F32), jnp.zeros((1, tq), F32))
    scores(0, sa_ref)
    stats = lax.fori_loop(0, last // 2, pair, (init, init))
    stats = lax.cond(last % 2 == 0, even_tail, odd_tail, stats)
    row = lax.broadcasted_iota(jnp.int32, (LANES, 1), 0)
    out = jnp.where(row < HEAD_DIM, acc_ref[0] / stats[0][1], acc_ref[1] / stats[1][1])
    o_ref[...] = out.T.astype(o_ref.dtype)


def _attn(qb, kaug, vt, *, batch, seq_len, tq, tk):
    n = qb.shape[0]
    nq = seq_len // tq
    kern = functools.partial(_attn_kernel, tq=tq, tk=tk)
    return pl.pallas_call(
        kern, grid=(batch, N_HEADS // 2, nq),
        in_specs=[
            pl.BlockSpec((tq, LANES), lambda b, p, i: (b * nq + i, p)),
            pl.BlockSpec((seq_len, 2 * LANES), lambda b, p, i: (b, p)),
            pl.BlockSpec((LANES, seq_len), lambda b, p, i: (p, b)),
        ],
        out_specs=pl.BlockSpec((tq, LANES), lambda b, p, i: (b * nq + i, p)),
        out_shape=jax.ShapeDtypeStruct((n, ATTN_W), BF16),
        scratch_shapes=[pltpu.VMEM((2, LANES, tq), F32), pltpu.VMEM((2, tk, tq), F32), pltpu.VMEM((2, tk, tq), F32)],
        compiler_params=_cparams(("arbitrary", "arbitrary", "arbitrary")), name="attn",
    )(qb, kaug, vt)


def _decode_query(q_ref):
    sub = lax.broadcasted_iota(jnp.int32, (N_HEADS, ATTN_W), 0)
    lane = lax.broadcasted_iota(jnp.int32, (N_HEADS, ATTN_W), 1)
    headmask = (lane // HEAD_DIM) == sub
    qbd = jnp.where(headmask, jnp.broadcast_to(q_ref[...].astype(F32), (N_HEADS, ATTN_W)), 0.0)
    return qbd.astype(BF16), headmask


def _decode_init(lfn_ref, m_ref, l_ref, acc_ref, carry_ref):
    m_ref[...] = jnp.full_like(m_ref, NEG_INF)
    l_ref[...] = jnp.zeros_like(l_ref)
    acc_ref[...] = jnp.zeros_like(acc_ref)
    carry_ref[...] = jnp.broadcast_to(lfn_ref[...], carry_ref.shape)


def _decode_pages(q_ref, su_ref, k_refs, v_refs, lf_refs, m_ref, l_ref, acc_ref, carry_ref):
    pp = len(k_refs)
    page = lf_refs[0].shape[1]
    qbd, _ = _decode_query(q_ref)
    parts = []
    for r in lf_refs:
        parts.extend(_split3(r[...]))
    sums = jnp.dot(jnp.concatenate(parts, axis=0), su_ref[...], preferred_element_type=F32)
    carry = carry_ref[...]
    ss = []
    for i in range(pp):
        base = 3 * N_HEADS * i
        both = sums[base:base + 8] + sums[base + 8:base + 16] + sums[base + 16:base + 24]
        s = jnp.dot(qbd, k_refs[i][...].astype(BF16), preferred_element_type=F32)
        ss.append(s + both[:, :page] + carry)
        carry = carry + both[:, page:]
    carry_ref[...] = carry
    m_old = m_ref[...]
    m = m_old
    for s in ss:
        m = jnp.maximum(m, s)
    alpha = jnp.exp(m_old - m)
    prs = [jnp.exp(s - m) for s in ss]
    l = alpha * l_ref[...]
    for pr in prs:
        l = l + pr
    for h in range(N_HEADS):
        rows = slice(h * HEAD_DIM, (h + 1) * HEAD_DIM)
        acc = acc_ref[rows, :] * alpha[h:h + 1, :]
        for i in range(pp):
            acc = acc + v_refs[i][rows, :] * prs[i][h:h + 1, :]
        acc_ref[rows, :] = acc
    m_ref[...] = m
    l_ref[...] = l


def _decode_finish(q_ref, kn_ref, vn_ref, o_ref, m_ref, l_ref, acc_ref):
    page = m_ref.shape[1]
    qbd, headmask = _decode_query(q_ref)
    m = m_ref[...]
    s_self = jnp.sum(qbd.astype(F32) * kn_ref[...].astype(F32), axis=-1, keepdims=True)
    m_all = jnp.maximum(jnp.max(m, axis=-1, keepdims=True), s_self)
    w = jnp.exp(m - m_all)
    w_self = jnp.exp(s_self - m_all)
    denom = jnp.sum(l_ref[...] * w, axis=-1, keepdims=True) + w_self
    ones = jnp.ones((N_HEADS, page), BF16)
    o = jnp.zeros((N_HEADS, ATTN_W), F32)
    weighted = jnp.concatenate(
        [acc_ref[h * HEAD_DIM:(h + 1) * HEAD_DIM, :] * w[h:h + 1, :] for h in range(N_HEADS)], axis=0)
    for part in _split3(weighted):
        o = o + lax.dot_general(ones, part, (((1,), (1,)), ((), ())), preferred_element_type=F32)
    o = o + w_self * jnp.broadcast_to(vn_ref[...].astype(F32), (N_HEADS, ATTN_W))
    o = jnp.where(headmask, o / denom, 0.0)
    o_ref[...] = jnp.sum(o, axis=0, keepdims=True).astype(o_ref.dtype)


class _DecodeWork:
    def __init__(self, page_table, qb, kb, vb, lf_new, cache_kt, cache_vt, cache_lf_t, pages_per_step):
        self.db, self.n_pages = page_table.shape
        self.pp = pages_per_step
        self.steps_per_seq = self.n_pages // self.pp
        self.n_steps = self.db * self.steps_per_seq
        page = cache_lf_t.shape[2]
        newer = (lax.broadcasted_iota(jnp.int32, (page, page), 0) >
                 lax.broadcasted_iota(jnp.int32, (page, page), 1))
        su = jnp.concatenate([newer.astype(BF16), jnp.ones((page, page), BF16)], axis=1)
        self.page_table = page_table
        self.operands = (qb, kb, vb, lf_new, su, cache_kt, cache_vt, cache_lf_t)
        self.page = page
        self.out_shape = jax.ShapeDtypeStruct((self.db, 1, ATTN_W), BF16)
        self.scratch = [pltpu.VMEM((N_HEADS, page), F32), pltpu.VMEM((N_HEADS, page), F32),
                        pltpu.VMEM((ATTN_W, page), F32), pltpu.VMEM((N_HEADS, page), F32),
                        pltpu.VMEM((2, self.pp, ATTN_W, page), F32), pltpu.VMEM((2, self.pp, ATTN_W, page), F32),
                        pltpu.VMEM((2, self.pp, N_HEADS, page), F32), pltpu.SemaphoreType.DMA((2, 3))]

    def specs(self, linear_step):
        page, sps = self.page, self.steps_per_seq

        def per_seq(shape):
            seq = lambda ids: jnp.minimum(linear_step(*ids), self.n_steps - 1) // sps
            return pl.BlockSpec((None,) + shape, lambda *a: (seq(a[:-1]), 0, 0))

        in_specs = [per_seq((1, ATTN_W)), per_seq((1, ATTN_W)), per_seq((1, ATTN_W)), per_seq((N_HEADS, 1)),
                    pl.BlockSpec((page, 2 * page), lambda *a: (0, 0))]
        in_specs += [pl.BlockSpec(memory_space=pl.ANY)] * 3
        return in_specs, per_seq((1, ATTN_W))

    def copies(self, pt_ref, step, slot, kt_hbm, vt_hbm, lf_hbm, kbuf, vbuf, lfbuf, sem):
        b = step // self.steps_per_seq
        j = step % self.steps_per_seq
        out = []
        for i in range(self.pp):
            pg = pt_ref[b, self.n_pages - 1 - (j * self.pp + i)]
            out.append(pltpu.make_async_copy(kt_hbm.at[pg], kbuf.at[slot, i], sem.at[slot, 0]))
            out.append(pltpu.make_async_copy(vt_hbm.at[pg], vbuf.at[slot, i], sem.at[slot, 1]))
            out.append(pltpu.make_async_copy(lf_hbm.at[pg], lfbuf.at[slot, i], sem.at[slot, 2]))
        return out


CONV_HALO = 32
CONV_ROWS = 32


def _ln_silu(y, g, b):
    mu = jnp.mean(y, axis=-1, keepdims=True)
    d = y - mu
    var = jnp.mean(d * d, axis=-1, keepdims=True)
    z = d * lax.rsqrt(var + EPS) * g + b
    return z * jax.nn.sigmoid(z)


def _conv_tile(u, first, w_ref, b_ref, g_ref, lb_ref, o_ref, xs_ref):
    t = u.shape[0]

    @pl.when(first)
    def _():
        xs_ref[0, 0:CONV_HALO, :] = jnp.zeros((CONV_HALO, CONV_CH), F32)

    @pl.when(jnp.logical_not(first))
    def _():
        xs_ref[0, 0:CONV_HALO, :] = xs_ref[0, t:t + CONV_HALO, :]

    xs_ref[0, CONV_HALO:, :] = u
    off = CONV_HALO - (CONV_W - 1)
    span = t + CONV_HALO - SUBLANES
    for r in range(1, SUBLANES):
        xs_ref[r, 0:span, :] = xs_ref[0, pl.ds(r, span), :]
    for c in range(t // CONV_ROWS):
        acc = jnp.broadcast_to(b_ref[...], (CONV_ROWS, CONV_CH))
        for w in range(CONV_W):
            r, base = (off + w) % SUBLANES, (off + w) // SUBLANES * SUBLANES
            acc = acc + xs_ref[r, pl.ds(c * CONV_ROWS + base, CONV_ROWS), :] * w_ref[w:w + 1, :]
        o_ref[c * CONV_ROWS:(c + 1) * CONV_ROWS, :] = _ln_silu(acc, g_ref[...], lb_ref[...]).astype(o_ref.dtype)


def _conv_step_kernel(xp_ref, w_ref, b_ref, g_ref, lb_ref, o_ref):
    acc = jnp.broadcast_to(b_ref[...], o_ref.shape)
    for w in range(CONV_W):
        acc = acc + xp_ref[w] * w_ref[w:w + 1, :]
    o_ref[...] = _ln_silu(acc, g_ref[...], lb_ref[...]).astype(o_ref.dtype)


def _conv_step(xp_t, conv_w, conv_b, ln_g, ln_b):
    db = xp_t.shape[1]
    args = (xp_t, conv_w, conv_b, ln_g, ln_b)
    return pl.pallas_call(
        _conv_step_kernel, grid=(1,),
        in_specs=[_full_spec(a.shape) for a in args],
        out_specs=_full_spec((db, CONV_CH)),
        out_shape=jax.ShapeDtypeStruct((db, CONV_CH), BF16),
        compiler_params=_cparams(("arbitrary",)), name="conv_step",
    )(*args)


GATE_PITCH = PEER_KEYS + SUBLANES


def _gate_grid(tok, sub, i1_ref, i2_ref, g_ref, stage_ref):
    i1 = i1_ref[pl.ds(tok, 1), :]
    i2 = i2_ref[pl.ds(tok, 1), :]
    g = g_ref[pl.ds(tok, 1), :]
    p1 = jnp.where(sub == i1, g, 0.0).astype(BF16)
    p2 = jnp.where(sub == i2, 1.0, 0.0).astype(BF16)
    grid = lax.dot_general(p1, p2, (((1,), (1,)), ((), ())), preferred_element_type=F32)
    stage_ref[pl.ds(pl.multiple_of(tok * GATE_PITCH, SUBLANES), PEER_KEYS), :] = grid


def _gate_rows(a, t, stage_ref, o_ref):
    rows = stage_ref[pl.ds(a, t, stride=GATE_PITCH), :]
    o_ref[:, pl.ds(pl.multiple_of(a * PEER_KEYS, PEER_KEYS), PEER_KEYS)] = rows.astype(o_ref.dtype)


N_CAND = 64
ROUTE_CHAINS = 4


def _top16(chains, iota_f):
    ss = [c[0] for c in chains]
    for r in range(PEER_TOPK):
        for n, (_, val_ref, idx_ref) in enumerate(chains):
            m = jnp.max(ss[n], axis=0, keepdims=True)
            idx = jnp.min(jnp.where(ss[n] == m, iota_f, float(1 << 20)), axis=0, keepdims=True)
            val_ref[r:r + 1, :] = m
            idx_ref[r:r + 1, :] = idx
            ss[n] = jnp.where(iota_f == idx, NEG_INF, ss[n])


def _batcher_network(n):
    pairs = []
    p = 1
    while p < n:
        k = p
        while k >= 1:
            for j in range(k % p, n - k, 2 * k):
                for i in range(min(k, n - j - k)):
                    if (i + j) // (2 * p) == (i + j + k) // (2 * p):
                        pairs.append((i + j, i + j + k))
            k //= 2
        p *= 2
    return pairs


def _top16_columns(srcs, outs):
    t = srcs[0].shape[1]
    groups = PEER_KEYS // SUBLANES
    sub = lax.broadcasted_iota(jnp.int32, (SUBLANES, t), 0).astype(F32)
    vals = [[s[SUBLANES * j:SUBLANES * (j + 1), :] for j in range(groups)] for s in srcs]
    idxs = [[sub + float(SUBLANES * j) for j in range(groups)] for _ in srcs]
    for a, b in _batcher_network(groups):
        for v, ix in zip(vals, idxs):
            swap = v[b] > v[a]
            v[a], v[b] = jnp.maximum(v[a], v[b]), jnp.minimum(v[a], v[b])
            ix[a], ix[b] = jnp.where(swap, ix[b], ix[a]), jnp.where(swap, ix[a], ix[b])
    flag = jnp.zeros((SUBLANES, t), F32)
    for v in vals:
        for d in range(groups - 1):
            flag = jnp.maximum(flag, jnp.where(v[d] == v[d + 1], 1.0, 0.0))
    for r in range(PEER_TOPK):
        for v, ix, (val_ref, idx_ref) in zip(vals, idxs, outs):
            m = jnp.max(v[0], axis=0, keepdims=True)
            idx = jnp.min(jnp.where(v[0] == m, ix[0], float(1 << 20)), axis=0, keepdims=True)
            val_ref[r:r + 1, :] = m
            idx_ref[r:r + 1, :] = idx
            won = ix[0] == idx
            for d in range(PEER_TOPK - 1 - r):
                v[d] = jnp.where(won, v[d + 1], v[d])
                ix[d] = jnp.where(won, ix[d + 1], ix[d])
    return flag


def _candidate_ids(t):
    row = lax.broadcasted_iota(jnp.int32, (N_CAND, t), 0)
    r1 = jnp.where(row < 16, 0, jnp.where(row < 40, (row - 8) // 8, jnp.where(row < 56, (row - 24) // 4, row - 48)))
    r2 = jnp.where(row < 16, row, jnp.where(row < 40, row % 8, jnp.where(row < 56, row % 4, 0)))
    return (r1 * PEER_TOPK + r2).astype(F32)


def _route_kernel(x_ref, a_ref, c_ref, woa_ref, woc_ref, g2_ref, wq_ref, sk_ref,
                  y_ref, h_ref, gd_ref,
                  sc_ref, tv_ref, ti_ref, cand_ref, bs_ref, o1_ref, o2_ref, og_ref, p1_ref, p2_ref, pg_ref, stage_ref):
    t = x_ref.shape[0]

    @pl.when(pl.program_id(0) == 0)
    def _():
        p1_ref[...] = jnp.zeros_like(p1_ref)
        p2_ref[...] = jnp.zeros_like(p2_ref)
        pg_ref[...] = jnp.zeros_like(pg_ref)

    y = (x_ref[...] + jnp.dot(a_ref[...], woa_ref[...], preferred_element_type=F32)
         + jnp.dot(c_ref[...], woc_ref[...], preferred_element_type=F32))
    y_ref[...] = y
    ms = jnp.mean(y * y, axis=-1, keepdims=True)
    h = (y * lax.rsqrt(ms + EPS) * g2_ref[...]).astype(BF16)
    h_ref[...] = h
    q = jnp.dot(h, wq_ref[...], preferred_element_type=F32).astype(BF16)
    for hp in range(2 * PEER_HEADS):
        sc_ref[hp] = lax.dot_general(sk_ref[hp], q[:, hp * PEER_KEYS:(hp + 1) * PEER_KEYS],
                                     (((1,), (1,)), ((), ())), preferred_element_type=F32)

    key_iota = lax.broadcasted_iota(jnp.int32, (PEER_KEYS, t), 0).astype(F32)

    pick_sub = lax.broadcasted_iota(jnp.int32, (PEER_KEYS, PEER_HEADS * PEER_TOPK), 0).astype(F32)
    tokens_per_trip = t // (2 * PEER_HEADS // ROUTE_CHAINS)

    def stage1(it, c):
        hps = [it * ROUTE_CHAINS + n for n in range(ROUTE_CHAINS)]
        outs = [(tv_ref.at[hp], ti_ref.at[hp]) for hp in hps]
        unordered_ties = _top16_columns([sc_ref[hp] for hp in hps], outs)
        for u in range(tokens_per_trip):
            _gate_grid(it * tokens_per_trip + u, pick_sub, p1_ref, p2_ref, pg_ref, stage_ref)

        @pl.when(jnp.max(unordered_ties) > 0.0)
        def _():
            _top16([(sc_ref[hp], val_ref, idx_ref) for hp, (val_ref, idx_ref) in zip(hps, outs)], key_iota)

        return c

    lax.fori_loop(0, 2 * PEER_HEADS // ROUTE_CHAINS, stage1, 0)

    cand_id = _candidate_ids(t)
    rank_iota = lax.broadcasted_iota(jnp.int32, (PEER_TOPK, t), 0).astype(F32)
    low4 = lax.broadcasted_iota(jnp.int32, (SUBLANES, t), 0) < 4

    def stage2(it, c):
        heads = [it * ROUTE_CHAINS + n for n in range(ROUTE_CHAINS)]
        ss, i1s, i2s = [], [], []
        for n, hd in enumerate(heads):
            v1 = tv_ref[2 * hd]
            v2 = tv_ref[2 * hd + 1]
            i1s.append(ti_ref[2 * hd])
            i2s.append(ti_ref[2 * hd + 1])
            v2lo = v2[0:SUBLANES, :]
            v2q = jnp.where(low4, v2lo, pltpu.roll(v2lo, 4, axis=0))
            cand_ref[n, 0:16, :] = v1[0:1, :] + v2
            for r1 in (1, 2, 3):
                cand_ref[n, 8 + 8 * r1:16 + 8 * r1, :] = v1[r1:r1 + 1, :] + v2lo
            cand_ref[n, 40:48, :] = jnp.where(low4, v1[4:5, :], v1[5:6, :]) + v2q
            cand_ref[n, 48:56, :] = jnp.where(low4, v1[6:7, :], v1[7:8, :]) + v2q
            cand_ref[n, 56:64, :] = v1[8:16, :] + v2[0:1, :]
            ss.append(cand_ref[n])
        for r in range(PEER_TOPK):
            for n, hd in enumerate(heads):
                base = pl.multiple_of(hd * PEER_TOPK, PEER_TOPK)
                m = jnp.max(ss[n], axis=0, keepdims=True)
                cid = jnp.min(jnp.where(ss[n] == m, cand_id, float(1 << 20)), axis=0, keepdims=True)
                ss[n] = jnp.where(cand_id == cid, NEG_INF, ss[n])
                r1 = jnp.floor(cid * (1.0 / PEER_TOPK))
                r2 = cid - r1 * PEER_TOPK
                bs_ref[n, r:r + 1, :] = m
                o1_ref[pl.ds(base + r, 1), :] = jnp.max(jnp.where(rank_iota == r1, i1s[n], -1.0), axis=0, keepdims=True)
                o2_ref[pl.ds(base + r, 1), :] = jnp.max(jnp.where(rank_iota == r2, i2s[n], -1.0), axis=0, keepdims=True)
        for n, hd in enumerate(heads):
            base = pl.multiple_of(hd * PEER_TOPK, PEER_TOPK)
            bs = bs_ref[n]
            e = jnp.exp(bs - bs[0:1, :])
            og_ref[pl.ds(base, PEER_TOPK), :] = e / jnp.sum(e, axis=0, keepdims=True)
        rows_per_trip = PEER_KEYS // (PEER_HEADS // ROUTE_CHAINS)
        for u in range(rows_per_trip):
            _gate_rows(it * rows_per_trip + u, t, stage_ref, gd_ref)
        return c

    lax.fori_loop(0, PEER_HEADS // ROUTE_CHAINS, stage2, 0)
    p1_ref[...] = o1_ref[...].T
    p2_ref[...] = o2_ref[...].T
    pg_ref[...] = og_ref[...].T


def _route(x2d, a2d, c2d, wts, *, tile):
    n = x2d.shape[0]
    nt = n // tile
    hk = PEER_HEADS * PEER_TOPK
    row = lambda w: pl.BlockSpec((tile, w), lambda i: (jnp.minimum(i, nt - 1), 0))
    out_shape = (jax.ShapeDtypeStruct((n, D_MODEL), F32),
                 jax.ShapeDtypeStruct((n, D_MODEL), BF16),
                 jax.ShapeDtypeStruct((n, N_EXPERTS), BF16))
    return pl.pallas_call(
        _route_kernel, grid=(nt + 1,),
        in_specs=[row(D_MODEL), row(ATTN_W), row(CONV_CH)] + [_full_spec(w.shape) for w in wts],
        out_specs=(row(D_MODEL), row(D_MODEL),
                   pl.BlockSpec((tile, N_EXPERTS), lambda i: (jnp.maximum(i - 1, 0), 0))),
        out_shape=out_shape,
        scratch_shapes=[pltpu.VMEM((2 * PEER_HEADS, PEER_KEYS, tile), F32),
                        pltpu.VMEM((2 * PEER_HEADS, PEER_TOPK, tile), F32),
                        pltpu.VMEM((2 * PEER_HEADS, PEER_TOPK, tile), F32),
                        pltpu.VMEM((ROUTE_CHAINS, N_CAND, tile), F32),
                        pltpu.VMEM((ROUTE_CHAINS, PEER_TOPK, tile), F32),
                        pltpu.VMEM((hk, tile), F32), pltpu.VMEM((hk, tile), F32), pltpu.VMEM((hk, tile), F32),
                        pltpu.VMEM((tile, hk), F32), pltpu.VMEM((tile, hk), F32), pltpu.VMEM((tile, hk), F32),
                        pltpu.VMEM((tile * GATE_PITCH, PEER_KEYS), F32)],
        compiler_params=_cparams(("arbitrary",)), name="route",
    )(x2d, a2d, c2d, *wts)


def _experts_kernel(pt_ref, h_ref, g_ref, u_ref, v_ref, y_ref, *rest, decode):
    if decode is None:
        o_ref, acc_ref, even_ref, odd_ref = rest
    else:
        q_ref, kn_ref, vn_ref, lfn_ref, su_ref, kt_hbm, vt_hbm, lf_hbm = rest[:8]
        (o_ref, od_ref, acc_ref, even_ref, odd_ref, m_ref, l_ref, dacc_ref, carry_ref,
         kbuf, vbuf, lfbuf, sem) = rest[8:]
        state = (m_ref, l_ref, dacc_ref, carry_ref)
        n_host = pl.num_programs(0) * pl.num_programs(1)
        step = pl.program_id(0) * pl.num_programs(1) + pl.program_id(1)
        active = step < decode.n_steps
        cur = jnp.minimum(step, decode.n_steps - 1)
        nxt = jnp.minimum(step + 1, decode.n_steps - 1)
        seq_step = cur % decode.steps_per_seq
        slot = step % 2
        copies = functools.partial(decode.copies, pt_ref, kt_hbm=kt_hbm, vt_hbm=vt_hbm, lf_hbm=lf_hbm,
                                   kbuf=kbuf, vbuf=vbuf, lfbuf=lfbuf, sem=sem)
    e = pl.program_id(1)

    @pl.when(e == 0)
    def _():
        acc_ref[...] = jnp.zeros_like(acc_ref)
        odd_ref[...] = jnp.zeros_like(odd_ref)

    if decode is not None:
        pl.when(active & (seq_step == 0))(lambda: _decode_init(lfn_ref, *state))

        @pl.when(step == 0)
        def _():
            for c in copies(0, 0):
                c.start()

    def work(cur_ref, prev_ref):
        if decode is not None:
            for c in copies(nxt, 1 - slot):
                c.start()
            for c in copies(cur, slot):
                c.wait()
            pages = lambda buf: [buf.at[slot, i] for i in range(decode.pp)]
            _decode_pages(q_ref, su_ref, pages(kbuf), pages(vbuf), pages(lfbuf), *state)
        h = h_ref[...]
        for c in range(u_ref.shape[0] // MXU_TILE):
            cols = slice(c * MXU_TILE, (c + 1) * MXU_TILE)
            s = lax.dot_general(h, u_ref[cols, :], (((1,), (1,)), ((), ())), preferred_element_type=F32)
            act = 0.5 * s * (1.0 + lax.erf(s * math.sqrt(0.5)))
            cur_ref[:, cols] = (g_ref[:, cols].astype(F32) * act).astype(BF16)
        prev = prev_ref[...]
        for c in range(v_ref.shape[1] // MXU_TILE):
            cols = slice(c * MXU_TILE, (c + 1) * MXU_TILE)
            acc_ref[:, cols] += jnp.dot(prev, v_ref[:, cols], preferred_element_type=F32)

    pl.when(e % 2 == 0)(lambda: work(even_ref, odd_ref))
    pl.when(e % 2 == 1)(lambda: work(odd_ref, even_ref))

    if decode is not None:
        pl.when(active & (seq_step == decode.steps_per_seq - 1))(
            lambda: _decode_finish(q_ref, kn_ref, vn_ref, od_ref, m_ref, l_ref, dacc_ref))

        @pl.when(step == n_host - 1)
        def _():
            for c in copies(nxt, 1 - slot):
                c.wait()

    @pl.when(e == pl.num_programs(1) - 1)
    def _():
        o_ref[...] = y_ref[...] + acc_ref[...]


def _experts(h2d, gdense, u_tab, v_tab, y2d, *, tile, eblk, decode=None):
    n = h2d.shape[0]
    nb = N_EXPERTS // eblk
    grid = (n // tile, nb + 1)
    scored = lambda e: jnp.minimum(e, nb - 1)
    folded = lambda e: jnp.maximum(e - 1, 0)
    in_specs = [pl.BlockSpec((tile, D_MODEL), lambda i, e, pt: (i, 0)),
                pl.BlockSpec((tile, eblk), lambda i, e, pt: (i, scored(e))),
                pl.BlockSpec((eblk, D_MODEL), lambda i, e, pt: (scored(e), 0)),
                pl.BlockSpec((eblk, D_MODEL), lambda i, e, pt: (folded(e), 0)),
                pl.BlockSpec((tile, D_MODEL), lambda i, e, pt: (i, 0))]
    out_specs = [pl.BlockSpec((tile, D_MODEL), lambda i, e, pt: (i, 0))]
    out_shape = [jax.ShapeDtypeStruct((n, D_MODEL), F32)]
    scratch = [pltpu.VMEM((tile, D_MODEL), F32), pltpu.VMEM((tile, eblk), BF16), pltpu.VMEM((tile, eblk), BF16)]
    operands = [h2d, gdense, u_tab, v_tab, y2d]
    if decode is None:
        prefetch = jnp.zeros((1, 1), jnp.int32)
    else:
        assert decode.n_steps <= grid[0] * grid[1], "not enough host steps for the paged attention"
        prefetch = decode.page_table
        dec_in, dec_out = decode.specs(lambda i, e: i * grid[1] + e)
        in_specs += dec_in
        out_specs.append(dec_out)
        out_shape.append(decode.out_shape)
        scratch += decode.scratch
        operands += list(decode.operands)
    grid_spec = pltpu.PrefetchScalarGridSpec(
        num_scalar_prefetch=1, grid=grid, in_specs=in_specs, out_specs=out_specs, scratch_shapes=scratch)
    outs = pl.pallas_call(
        functools.partial(_experts_kernel, decode=decode), grid_spec=grid_spec, out_shape=out_shape,
        compiler_params=_cparams(("arbitrary", "arbitrary")), name="experts",
    )(prefetch, *operands)
    return outs[0] if decode is None else tuple(outs)


PROJ_TILE = 512
ATTN_TILE = 512
ATTN_KEY_TILE = 512
ROUTE_TILE = 128
EXPERT_TILE = 1024
EXPERT_BLOCK = 512
DECODE_PAGES_PER_STEP = 8


def _finish(x2d, a2d, c2d, route_wts, u_tab, v_tab, *, expert_tile, decode=None):
    n = x2d.shape[0]
    y, h2, gd = _route(x2d, a2d, c2d, route_wts, tile=min(ROUTE_TILE, n))
    return _experts(h2, gd, u_tab, v_tab, y, tile=expert_tile, eblk=EXPERT_BLOCK, decode=decode)


def kernel(x_prompt, x_sample, cache_k, cache_v, cache_logf, state_conv, page_table, norm1_g, w_in, b_forget,
           q_gain, k_gain, conv_w, conv_b, conv_ln_g, conv_ln_b, w_out, norm2_g, peer_wq, peer_subkeys,
           peer_u, peer_v):
    assert w_in.shape[0] == 1, "single layer"
    batch, seq_len, _ = x_prompt.shape
    db = x_sample.shape[0]
    n_pool, page = cache_k.shape[1], cache_k.shape[2]

    w = w_in[0].astype(BF16)
    w_qkv = w[:, :3 * ATTN_W]
    w_ag = w[:, 3 * ATTN_W:3 * ATTN_W + 2 * CONV_CH]
    w_fg = jnp.pad(w[:, 3 * ATTN_W + 2 * CONV_CH:], ((0, 0), (0, LANES - N_HEADS)))
    b_fg = jnp.pad(b_forget[0].reshape(1, N_HEADS), ((0, 0), (0, LANES - N_HEADS)))
    qg = jnp.tile(q_gain[0], N_HEADS).reshape(1, ATTN_W)
    kg = jnp.tile(k_gain[0], N_HEADS).reshape(1, ATTN_W)
    gi = lax.broadcasted_iota(jnp.int32, (ATTN_W, ATTN_W), 0) // HEAD_DIM
    gj = lax.broadcasted_iota(jnp.int32, (ATTN_W, ATTN_W), 1) // HEAD_DIM
    bd = (gi == gj).astype(BF16)
    tri = (lax.broadcasted_iota(jnp.int32, (LANES, LANES), 1) <=
           lax.broadcasted_iota(jnp.int32, (LANES, LANES), 0)).astype(BF16)
    src = lax.broadcasted_iota(jnp.int32, (ATTN_W, N_HEADS * LANES), 0)
    dst = lax.broadcasted_iota(jnp.int32, (ATTN_W, N_HEADS * LANES), 1)
    place_k = (dst == (src // HEAD_DIM) * LANES + src % HEAD_DIM).astype(BF16)
    src = lax.broadcasted_iota(jnp.int32, (N_BIAS_TERMS * LANES, N_HEADS * LANES), 0)
    dst = lax.broadcasted_iota(jnp.int32, (N_BIAS_TERMS * LANES, N_HEADS * LANES), 1)
    place_c = ((src % LANES < N_HEADS) & (dst == (src % LANES) * LANES + HEAD_DIM + src // LANES)).astype(BF16)
    conv_wts = (conv_w[0], conv_b[0].reshape(1, CONV_CH), conv_ln_g[0].reshape(1, CONV_CH),
                conv_ln_b[0].reshape(1, CONV_CH))
    proj_wts = (norm1_g[0].reshape(1, D_MODEL), w_qkv, w_ag, w_fg, b_fg, qg, kg, bd, tri, place_k, place_c, *conv_wts)
    wo = w_out[0].astype(BF16)
    sk = peer_subkeys[0].astype(BF16).reshape(2 * PEER_HEADS, PEER_KEYS, PEER_KEYS)
    route_wts = (wo[:ATTN_W], wo[ATTN_W:], norm2_g[0].reshape(1, D_MODEL), peer_wq[0].astype(BF16), sk)
    u_tab = peer_u[0].astype(BF16)
    v_tab = peer_v[0].astype(BF16)

    xp2 = x_prompt.reshape(batch * seq_len, D_MODEL)
    k_p, v_p, u_p, lf_p, qb, _, _, kaug, vt, c_p = _proj(
        xp2, proj_wts, tile=PROJ_TILE, seq_len=seq_len, with_cumsum=True)
    xs2 = x_sample.reshape(db, D_MODEL)
    k_s, v_s, u_s, lf_s, qsb, ksb, vsb, _, _, _ = _proj(xs2, proj_wts, tile=db, seq_len=db, with_cumsum=False)

    lf_t = cache_logf[0].transpose(0, 2, 1)
    row = lambda a: a.reshape(db, 1, ATTN_W)
    transposed = lambda c: c[0].transpose(0, 2, 3, 1).reshape(n_pool, ATTN_W, page)
    decode = _DecodeWork(page_table, row(qsb), row(ksb), row(vsb), lf_s.reshape(db, N_HEADS, 1),
                         transposed(cache_k), transposed(cache_v), lf_t, DECODE_PAGES_PER_STEP)

    a_p = _attn(qb, kaug, vt, batch=batch, seq_len=seq_len, tq=ATTN_TILE, tk=ATTN_KEY_TILE)
    y_p, a_s = _finish(xp2, a_p, c_p, route_wts, u_tab, v_tab, expert_tile=EXPERT_TILE, decode=decode)

    window = jnp.concatenate([state_conv[0], u_s[:, None, :]], axis=1)
    c_s = _conv_step(window.transpose(1, 0, 2), *conv_wts)
    y_s = _finish(xs2, a_s.reshape(db, ATTN_W), c_s, route_wts, u_tab, v_tab, expert_tile=db)

    tail = CONV_W - 1
    state = lambda t, seqs, toks: t.reshape(seqs, N_HEADS, HEAD_DIM, toks).transpose(0, 3, 1, 2)[None]
    return (y_p.reshape(batch, seq_len, D_MODEL),
            y_s.reshape(db, 1, D_MODEL),
            state(k_p, batch, seq_len),
            state(v_p, batch, seq_len),
            lf_p.reshape(1, batch, seq_len, N_HEADS),
            u_p.reshape(batch, seq_len, CONV_CH)[:, seq_len - tail:][None],
            state(k_s, 1, db).reshape(1, db, 1, N_HEADS, HEAD_DIM),
            state(v_s, 1, db).reshape(1, db, 1, N_HEADS, HEAD_DIM),
            lf_s.reshape(1, db, 1, N_HEADS),
            window[:, 1:][None])
```

```python
import functools
import math

import jax
import jax.numpy as jnp
from jax import lax
from jax.experimental import pallas as pl
from jax.experimental.pallas import tpu as pltpu

F32 = jnp.float32
BF16 = jnp.bfloat16

D_MODEL = 1024
HEAD_DIM = 64
N_HEADS = 8
ATTN_W = N_HEADS * HEAD_DIM
CONV_CH = D_MODEL - ATTN_W
CONV_W = 31
PEER_HEADS = 8
PEER_KEYS = 128
PEER_TOPK = 16
N_EXPERTS = PEER_KEYS * PEER_KEYS
EPS = 1e-6
LANES = 128
SUBLANES = 8
MXU_TILE = 256
VMEM_LIMIT = 56 * 1024 * 1024
NEG_INF = float("-inf")


def _cparams(sem):
    return pltpu.CompilerParams(dimension_semantics=sem, vmem_limit_bytes=VMEM_LIMIT)


def _split3(x):
    p1 = x.astype(BF16)
    r1 = x - p1.astype(F32)
    p2 = r1.astype(BF16)
    p3 = (r1 - p2.astype(F32)).astype(BF16)
    return p1, p2, p3


def _full_spec(shape):
    return pl.BlockSpec(shape, lambda *_: (0,) * len(shape))


def _group_rms(t, bd, gain):
    sq = t * t
    hi = sq.astype(BF16)
    lo = (sq - hi.astype(F32)).astype(BF16)
    ss = jnp.dot(hi, bd, preferred_element_type=F32) + jnp.dot(lo, bd, preferred_element_type=F32)
    return t * lax.rsqrt(ss * (1.0 / HEAD_DIM) + EPS) * gain


def _proj_kernel(x_ref, g1_ref, wqkv_ref, wag_ref, wfg_ref, bf_ref, qg_ref, kg_ref, bd_ref, tri_ref, pk_ref, pc_ref,
                 cw_ref, cb_ref, cg_ref, clb_ref,
                 k_ref, v_ref, u_ref, lf_ref, qb_ref, kb_ref, vb_ref, kaug_ref, vt_ref, c_ref,
                 carry_ref, xs_ref, *, tiles_per_seq, with_cumsum):
    t = x_ref.shape[0]
    x = x_ref[...]
    ms = jnp.mean(x * x, axis=-1, keepdims=True)
    h = (x * lax.rsqrt(ms + EPS) * g1_ref[...]).astype(BF16)

    qkv = jnp.dot(h, wqkv_ref[...], preferred_element_type=F32)
    bd = bd_ref[...]
    q = _group_rms(qkv[:, :ATTN_W], bd, qg_ref[...])
    k = _group_rms(qkv[:, ATTN_W:2 * ATTN_W], bd, kg_ref[...])
    v = qkv[:, 2 * ATTN_W:]
    v_t = v.T
    k_ref[...] = k.T
    v_ref[...] = v_t
    qb_ref[...] = (q * (HEAD_DIM ** -0.5)).astype(BF16)
    kb_ref[...] = k.astype(BF16)
    vb_ref[...] = v.astype(BF16)

    ag = jnp.dot(h, wag_ref[...], preferred_element_type=F32)
    u = ag[:, :CONV_CH] * jax.nn.sigmoid(ag[:, CONV_CH:])
    u_ref[...] = u

    z = jnp.dot(h, wfg_ref[...], preferred_element_type=F32) + bf_ref[...]
    logf = jnp.minimum(z, 0.0) - jnp.log1p(jnp.exp(-jnp.abs(z)))
    lf_ref[...] = logf[:, :N_HEADS]

    if with_cumsum:
        vt_ref[...] = v_t.astype(BF16)
        i = pl.program_id(0)
        _conv_tile(u, i % tiles_per_seq == 0, cw_ref, cb_ref, cg_ref, clb_ref, c_ref, xs_ref)
        lane = lax.broadcasted_iota(jnp.int32, (1, LANES), 1)
        carry = jnp.where(i % tiles_per_seq == 0, 0.0, carry_ref[...])
        tri = tri_ref[...]
        for sb in range(t // LANES):
            rows = slice(sb * LANES, (sb + 1) * LANES)
            blk = jnp.where(lane < N_HEADS, logf[rows], 0.0)
            p1, p2, p3 = _split3(blk)
            cs = (jnp.dot(tri, p1, preferred_element_type=F32)
                  + jnp.dot(tri, p2, preferred_element_type=F32)
                  + jnp.dot(tri, p3, preferred_element_type=F32)) + carry
            carry = cs[LANES - 1:LANES, :]
            placed = (jnp.dot(k[rows].astype(BF16), pk_ref[...], preferred_element_type=F32)
                      + jnp.dot(jnp.concatenate(_split3(-cs), axis=1), pc_ref[...], preferred_element_type=F32))
            kaug_ref[rows, :] = placed.astype(BF16)
        carry_ref[...] = carry
    else:
        kaug_ref[...] = jnp.zeros_like(kaug_ref)
        vt_ref[...] = jnp.zeros_like(vt_ref)
        c_ref[...] = jnp.zeros_like(c_ref)


def _proj(x2d, wts, *, tile, seq_len, with_cumsum):
    n = x2d.shape[0]
    nt = n // tile
    row = lambda w: pl.BlockSpec((tile, w), lambda i: (i, 0))
    tps = seq_len // tile
    kern = functools.partial(_proj_kernel, tiles_per_seq=tps, with_cumsum=with_cumsum)
    state_t = pl.BlockSpec((ATTN_W, tile), lambda i: (i // tps, i % tps))
    out_shape = (
        jax.ShapeDtypeStruct((n // seq_len * ATTN_W, seq_len), F32),
        jax.ShapeDtypeStruct((n // seq_len * ATTN_W, seq_len), F32),
        jax.ShapeDtypeStruct((n, CONV_CH), F32),
        jax.ShapeDtypeStruct((n, N_HEADS), F32),
        jax.ShapeDtypeStruct((n, ATTN_W), BF16),
        jax.ShapeDtypeStruct((n, ATTN_W), BF16),
        jax.ShapeDtypeStruct((n, ATTN_W), BF16),
        jax.ShapeDtypeStruct((n, N_HEADS * LANES), BF16),
        jax.ShapeDtypeStruct((ATTN_W, n), BF16),
        jax.ShapeDtypeStruct((n, CONV_CH), BF16),
    )
    out_specs = (state_t, state_t, row(CONV_CH), row(N_HEADS), row(ATTN_W), row(ATTN_W), row(ATTN_W),
                 row(N_HEADS * LANES), pl.BlockSpec((ATTN_W, tile), lambda i: (0, i)), row(CONV_CH))
    in_specs = [row(D_MODEL)] + [_full_spec(w.shape) for w in wts]
    return pl.pallas_call(
        kern, grid=(nt,), in_specs=in_specs, out_specs=out_specs, out_shape=out_shape,
        scratch_shapes=[pltpu.VMEM((1, LANES), F32),
                        pltpu.VMEM((SUBLANES, max(tile, CONV_HALO) + CONV_HALO, CONV_CH), F32)],
        compiler_params=_cparams(("arbitrary",)), name="proj",
    )(x2d, *wts)


N_BIAS_TERMS = 3


def _attn_kernel(q_ref, kaug_ref, vt_ref, o_ref, acc_ref, sa_ref, sb_ref, *, tq, tk):
    qi = pl.program_id(2)
    lane = lax.broadcasted_iota(jnp.int32, (1, LANES), 1)
    q = q_ref[...].astype(F32)
    ones = jnp.where(lane < HEAD_DIM + N_BIAS_TERMS, 1.0, 0.0)
    key_row = lax.broadcasted_iota(jnp.int32, (tk, tq), 0)
    qry_col = lax.broadcasted_iota(jnp.int32, (tk, tq), 1)
    last = (qi * tq) // tk
    visible = key_row + last * tk <= qry_col + qi * tq
    qhs = []
    for hh in range(2):
        qh = q if hh == 0 else pltpu.roll(q, HEAD_DIM, axis=1)
        qhs.append(jnp.where(lane < HEAD_DIM, qh, ones).astype(BF16))
    acc_ref[...] = jnp.zeros_like(acc_ref)

    def scores(j, s_ref):
        start = pl.multiple_of(j * tk, tk)
        for hh in range(2):
            s_ref[hh] = lax.dot_general(kaug_ref[pl.ds(start, tk), hh * LANES:(hh + 1) * LANES], qhs[hh],
                                        (((1,), (1,)), ((), ())), preferred_element_type=F32)

    def consume(j, s_ref, stats, masked):
        start = pl.multiple_of(j * tk, tk)
        vt = vt_ref[:, pl.ds(start, tk)]
        new_stats = []
        for hh in range(2):
            m, l = stats[hh]
            s = jnp.where(visible, s_ref[hh], NEG_INF) if masked else s_ref[hh]
            m_new = jnp.maximum(m, jnp.max(s, axis=0, keepdims=True))
            alpha = jnp.exp(m - m_new)
            pr = jnp.exp(s - m_new)
            l = alpha * l + jnp.sum(pr, axis=0, keepdims=True)
            acc_ref[hh] = alpha * acc_ref[hh] + jnp.dot(vt, pr.astype(BF16), preferred_element_type=F32)
            new_stats.append((m_new, l))
        return tuple(new_stats)

    def pair(jj, stats):
        j = 2 * jj
        scores(j + 1, sb_ref)
        stats = consume(j, sa_ref, stats, False)
        scores(j + 2, sa_ref)
        return consume(j + 1, sb_ref, stats, False)

    def even_tail(stats):
        return consume(last, sa_ref, stats, True)

    def odd_tail(stats):
        scores(last, sb_ref)
        stats = consume(last - 1, sa_ref, stats, False)
        return consume(last, sb_ref, stats, True)

    init = (jnp.full((1, tq), NEG_INF, F32), jnp.zeros((1, tq), F32))
    scores(0, sa_ref)
    stats = lax.fori_loop(0, last // 2, pair, (init, init))
    stats = lax.cond(last % 2 == 0, even_tail, odd_tail, stats)
    row = lax.broadcasted_iota(jnp.int32, (LANES, 1), 0)
    out = jnp.where(row < HEAD_DIM, acc_ref[0] / stats[0][1], acc_ref[1] / stats[1][1])
    o_ref[...] = out.T.astype(o_ref.dtype)


def _attn(qb, kaug, vt, *, batch, seq_len, tq, tk):
    n = qb.shape[0]
    nq = seq_len // tq
    kern = functools.partial(_attn_kernel, tq=tq, tk=tk)
    return pl.pallas_call(
        kern, grid=(batch, N_HEADS // 2, nq),
        in_specs=[
            pl.BlockSpec((tq, LANES), lambda b, p, i: (b * nq + i, p)),
            pl.BlockSpec((seq_len, 2 * LANES), lambda b, p, i: (b, p)),
            pl.BlockSpec((LANES, seq_len), lambda b, p, i: (p, b)),
        ],
        out_specs=pl.BlockSpec((tq, LANES), lambda b, p, i: (b * nq + i, p)),
        out_shape=jax.ShapeDtypeStruct((n, ATTN_W), BF16),
        scratch_shapes=[pltpu.VMEM((2, LANES, tq), F32), pltpu.VMEM((2, tk, tq), F32), pltpu.VMEM((2, tk, tq), F32)],
        compiler_params=_cparams(("arbitrary", "arbitrary", "arbitrary")), name="attn",
    )(qb, kaug, vt)


def _decode_query(q_ref):
    sub = lax.broadcasted_iota(jnp.int32, (N_HEADS, ATTN_W), 0)
    lane = lax.broadcasted_iota(jnp.int32, (N_HEADS, ATTN_W), 1)
    headmask = (lane // HEAD_DIM) == sub
    qbd = jnp.where(headmask, jnp.broadcast_to(q_ref[...].astype(F32), (N_HEADS, ATTN_W)), 0.0)
    return qbd.astype(BF16), headmask


def _decode_init(lfn_ref, m_ref, l_ref, acc_ref, carry_ref):
    m_ref[...] = jnp.full_like(m_ref, NEG_INF)
    l_ref[...] = jnp.zeros_like(l_ref)
    acc_ref[...] = jnp.zeros_like(acc_ref)
    carry_ref[...] = jnp.broadcast_to(lfn_ref[...], carry_ref.shape)


def _decode_pages(q_ref, su_ref, k_refs, v_refs, lf_refs, m_ref, l_ref, acc_ref, carry_ref):
    pp = len(k_refs)
    page = lf_refs[0].shape[1]
    qbd, _ = _decode_query(q_ref)
    parts = []
    for r in lf_refs:
        parts.extend(_split3(r[...]))
    sums = jnp.dot(jnp.concatenate(parts, axis=0), su_ref[...], preferred_element_type=F32)
    carry = carry_ref[...]
    ss = []
    for i in range(pp):
        base = 3 * N_HEADS * i
        both = sums[base:base + 8] + sums[base + 8:base + 16] + sums[base + 16:base + 24]
        s = jnp.dot(qbd, k_refs[i][...].astype(BF16), preferred_element_type=F32)
        ss.append(s + both[:, :page] + carry)
        carry = carry + both[:, page:]
    carry_ref[...] = carry
    m_old = m_ref[...]
    m = m_old
    for s in ss:
        m = jnp.maximum(m, s)
    alpha = jnp.exp(m_old - m)
    prs = [jnp.exp(s - m) for s in ss]
    l = alpha * l_ref[...]
    for pr in prs:
        l = l + pr
    for h in range(N_HEADS):
        rows = slice(h * HEAD_DIM, (h + 1) * HEAD_DIM)
        acc = acc_ref[rows, :] * alpha[h:h + 1, :]
        for i in range(pp):
            acc = acc + v_refs[i][rows, :] * prs[i][h:h + 1, :]
        acc_ref[rows, :] = acc
    m_ref[...] = m
    l_ref[...] = l


def _decode_finish(q_ref, kn_ref, vn_ref, o_ref, m_ref, l_ref, acc_ref):
    page = m_ref.shape[1]
    qbd, headmask = _decode_query(q_ref)
    m = m_ref[...]
    s_self = jnp.sum(qbd.astype(F32) * kn_ref[...].astype(F32), axis=-1, keepdims=True)
    m_all = jnp.maximum(jnp.max(m, axis=-1, keepdims=True), s_self)
    w = jnp.exp(m - m_all)
    w_self = jnp.exp(s_self - m_all)
    denom = jnp.sum(l_ref[...] * w, axis=-1, keepdims=True) + w_self
    ones = jnp.ones((N_HEADS, page), BF16)
    o = jnp.zeros((N_HEADS, ATTN_W), F32)
    weighted = jnp.concatenate(
        [acc_ref[h * HEAD_DIM:(h + 1) * HEAD_DIM, :] * w[h:h + 1, :] for h in range(N_HEADS)], axis=0)
    for part in _split3(weighted):
        o = o + lax.dot_general(ones, part, (((1,), (1,)), ((), ())), preferred_element_type=F32)
    o = o + w_self * jnp.broadcast_to(vn_ref[...].astype(F32), (N_HEADS, ATTN_W))
    o = jnp.where(headmask, o / denom, 0.0)
    o_ref[...] = jnp.sum(o, axis=0, keepdims=True).astype(o_ref.dtype)


PAGE_COPY_PRIORITY = 1


class _DecodeWork:
    def __init__(self, page_table, qb, kb, vb, lf_new, cache_kt, cache_vt, cache_lf_t, pages_per_step):
        self.db, self.n_pages = page_table.shape
        self.pp = pages_per_step
        self.steps_per_seq = self.n_pages // self.pp
        self.n_steps = self.db * self.steps_per_seq
        page = cache_lf_t.shape[2]
        newer = (lax.broadcasted_iota(jnp.int32, (page, page), 0) >
                 lax.broadcasted_iota(jnp.int32, (page, page), 1))
        su = jnp.concatenate([newer.astype(BF16), jnp.ones((page, page), BF16)], axis=1)
        self.page_table = page_table
        self.operands = (qb, kb, vb, lf_new, su, cache_kt, cache_vt, cache_lf_t)
        self.page = page
        self.out_shape = jax.ShapeDtypeStruct((self.db, 1, ATTN_W), BF16)
        self.scratch = [pltpu.VMEM((N_HEADS, page), F32), pltpu.VMEM((N_HEADS, page), F32),
                        pltpu.VMEM((ATTN_W, page), F32), pltpu.VMEM((N_HEADS, page), F32),
                        pltpu.VMEM((2, self.pp, ATTN_W, page), F32), pltpu.VMEM((2, self.pp, ATTN_W, page), F32),
                        pltpu.VMEM((2, self.pp, N_HEADS, page), F32), pltpu.SemaphoreType.DMA((2, 3))]

    def specs(self, linear_step):
        page, sps = self.page, self.steps_per_seq

        def per_seq(shape):
            seq = lambda ids: jnp.minimum(linear_step(*ids), self.n_steps - 1) // sps
            return pl.BlockSpec((None,) + shape, lambda *a: (seq(a[:-1]), 0, 0))

        in_specs = [per_seq((1, ATTN_W)), per_seq((1, ATTN_W)), per_seq((1, ATTN_W)), per_seq((N_HEADS, 1)),
                    pl.BlockSpec((page, 2 * page), lambda *a: (0, 0))]
        in_specs += [pl.BlockSpec(memory_space=pl.ANY)] * 3
        return in_specs, per_seq((1, ATTN_W))

    def copies(self, pt_ref, step, slot, kt_hbm, vt_hbm, lf_hbm, kbuf, vbuf, lfbuf, sem):
        b = step // self.steps_per_seq
        j = step % self.steps_per_seq
        out = []
        for i in range(self.pp):
            pg = pt_ref[b, self.n_pages - 1 - (j * self.pp + i)]
            out.append(pltpu.make_async_copy(kt_hbm.at[pg], kbuf.at[slot, i], sem.at[slot, 0]))
            out.append(pltpu.make_async_copy(vt_hbm.at[pg], vbuf.at[slot, i], sem.at[slot, 1]))
            out.append(pltpu.make_async_copy(lf_hbm.at[pg], lfbuf.at[slot, i], sem.at[slot, 2]))
        return out


CONV_HALO = 32
CONV_ROWS = 32


def _ln_silu(y, g, b):
    mu = jnp.mean(y, axis=-1, keepdims=True)
    d = y - mu
    var = jnp.mean(d * d, axis=-1, keepdims=True)
    z = d * lax.rsqrt(var + EPS) * g + b
    return z * jax.nn.sigmoid(z)


def _conv_tile(u, first, w_ref, b_ref, g_ref, lb_ref, o_ref, xs_ref):
    t = u.shape[0]

    @pl.when(first)
    def _():
        xs_ref[0, 0:CONV_HALO, :] = jnp.zeros((CONV_HALO, CONV_CH), F32)

    @pl.when(jnp.logical_not(first))
    def _():
        xs_ref[0, 0:CONV_HALO, :] = xs_ref[0, t:t + CONV_HALO, :]

    xs_ref[0, CONV_HALO:, :] = u
    off = CONV_HALO - (CONV_W - 1)
    span = t + CONV_HALO - SUBLANES
    for r in range(1, SUBLANES):
        xs_ref[r, 0:span, :] = xs_ref[0, pl.ds(r, span), :]
    for c in range(t // CONV_ROWS):
        acc = jnp.broadcast_to(b_ref[...], (CONV_ROWS, CONV_CH))
        for w in range(CONV_W):
            r, base = (off + w) % SUBLANES, (off + w) // SUBLANES * SUBLANES
            acc = acc + xs_ref[r, pl.ds(c * CONV_ROWS + base, CONV_ROWS), :] * w_ref[w:w + 1, :]
        o_ref[c * CONV_ROWS:(c + 1) * CONV_ROWS, :] = _ln_silu(acc, g_ref[...], lb_ref[...]).astype(o_ref.dtype)


def _conv_step_kernel(xp_ref, w_ref, b_ref, g_ref, lb_ref, o_ref):
    acc = jnp.broadcast_to(b_ref[...], o_ref.shape)
    for w in range(CONV_W):
        acc = acc + xp_ref[w] * w_ref[w:w + 1, :]
    o_ref[...] = _ln_silu(acc, g_ref[...], lb_ref[...]).astype(o_ref.dtype)


def _conv_step(xp_t, conv_w, conv_b, ln_g, ln_b):
    db = xp_t.shape[1]
    args = (xp_t, conv_w, conv_b, ln_g, ln_b)
    return pl.pallas_call(
        _conv_step_kernel, grid=(1,),
        in_specs=[_full_spec(a.shape) for a in args],
        out_specs=_full_spec((db, CONV_CH)),
        out_shape=jax.ShapeDtypeStruct((db, CONV_CH), BF16),
        compiler_params=_cparams(("arbitrary",)), name="conv_step",
    )(*args)


GATE_PITCH = PEER_KEYS + SUBLANES


def _gate_grid(tok, sub, i1_ref, i2_ref, g_ref, stage_ref):
    i1 = i1_ref[pl.ds(tok, 1), :]
    i2 = i2_ref[pl.ds(tok, 1), :]
    g = g_ref[pl.ds(tok, 1), :]
    p1 = jnp.where(sub == i1, g, 0.0).astype(BF16)
    p2 = jnp.where(sub == i2, 1.0, 0.0).astype(BF16)
    grid = lax.dot_general(p1, p2, (((1,), (1,)), ((), ())), preferred_element_type=F32)
    stage_ref[pl.ds(pl.multiple_of(tok * GATE_PITCH, SUBLANES), PEER_KEYS), :] = grid


def _gate_rows(a, t, stage_ref, o_ref):
    rows = stage_ref[pl.ds(a, t, stride=GATE_PITCH), :]
    o_ref[:, pl.ds(pl.multiple_of(a * PEER_KEYS, PEER_KEYS), PEER_KEYS)] = rows.astype(o_ref.dtype)


N_CAND = 64
ROUTE_CHAINS = 4


def _top16(chains, iota_f):
    ss = [c[0] for c in chains]
    for r in range(PEER_TOPK):
        for n, (_, val_ref, idx_ref) in enumerate(chains):
            m = jnp.max(ss[n], axis=0, keepdims=True)
            idx = jnp.min(jnp.where(ss[n] == m, iota_f, float(1 << 20)), axis=0, keepdims=True)
            val_ref[r:r + 1, :] = m
            idx_ref[r:r + 1, :] = idx
            ss[n] = jnp.where(iota_f == idx, NEG_INF, ss[n])


def _batcher_network(n):
    pairs = []
    p = 1
    while p < n:
        k = p
        while k >= 1:
            for j in range(k % p, n - k, 2 * k):
                for i in range(min(k, n - j - k)):
                    if (i + j) // (2 * p) == (i + j + k) // (2 * p):
                        pairs.append((i + j, i + j + k))
            k //= 2
        p *= 2
    return pairs


def _top16_columns(srcs, outs):
    t = srcs[0].shape[1]
    groups = PEER_KEYS // SUBLANES
    sub = lax.broadcasted_iota(jnp.int32, (SUBLANES, t), 0).astype(F32)
    vals = [[s[SUBLANES * j:SUBLANES * (j + 1), :] for j in range(groups)] for s in srcs]
    idxs = [[sub + float(SUBLANES * j) for j in range(groups)] for _ in srcs]
    for a, b in _batcher_network(groups):
        for v, ix in zip(vals, idxs):
            swap = v[b] > v[a]
            v[a], v[b] = jnp.maximum(v[a], v[b]), jnp.minimum(v[a], v[b])
            ix[a], ix[b] = jnp.where(swap, ix[b], ix[a]), jnp.where(swap, ix[a], ix[b])
    flag = jnp.zeros((SUBLANES, t), F32)
    for v in vals:
        for d in range(groups - 1):
            flag = jnp.maximum(flag, jnp.where(v[d] == v[d + 1], 1.0, 0.0))
    for r in range(PEER_TOPK):
        for v, ix, (val_ref, idx_ref) in zip(vals, idxs, outs):
            m = jnp.max(v[0], axis=0, keepdims=True)
            idx = jnp.min(jnp.where(v[0] == m, ix[0], float(1 << 20)), axis=0, keepdims=True)
            val_ref[r:r + 1, :] = m
            idx_ref[r:r + 1, :] = idx
            won = ix[0] == idx
            for d in range(PEER_TOPK - 1 - r):
                v[d] = jnp.where(won, v[d + 1], v[d])
                ix[d] = jnp.where(won, ix[d + 1], ix[d])
    return flag


def _candidate_ids(t):
    row = lax.broadcasted_iota(jnp.int32, (N_CAND, t), 0)
    r1 = jnp.where(row < 16, 0, jnp.where(row < 40, (row - 8) // 8, jnp.where(row < 56, (row - 24) // 4, row - 48)))
    r2 = jnp.where(row < 16, row, jnp.where(row < 40, row % 8, jnp.where(row < 56, row % 4, 0)))
    return (r1 * PEER_TOPK + r2).astype(F32)


def _route_kernel(x_ref, a_ref, c_ref, woa_ref, woc_ref, g2_ref, wq_ref, sk_ref,
                  y_ref, h_ref, gd_ref,
                  sc_ref, tv_ref, ti_ref, cand_ref, bs_ref, o1_ref, o2_ref, og_ref, p1_ref, p2_ref, pg_ref, stage_ref):
    t = x_ref.shape[0]

    @pl.when(pl.program_id(0) == 0)
    def _():
        p1_ref[...] = jnp.zeros_like(p1_ref)
        p2_ref[...] = jnp.zeros_like(p2_ref)
        pg_ref[...] = jnp.zeros_like(pg_ref)

    y = (x_ref[...] + jnp.dot(a_ref[...], woa_ref[...], preferred_element_type=F32)
         + jnp.dot(c_ref[...], woc_ref[...], preferred_element_type=F32))
    y_ref[...] = y
    ms = jnp.mean(y * y, axis=-1, keepdims=True)
    h = (y * lax.rsqrt(ms + EPS) * g2_ref[...]).astype(BF16)
    h_ref[...] = h
    q = jnp.dot(h, wq_ref[...], preferred_element_type=F32).astype(BF16)
    for hp in range(2 * PEER_HEADS):
        sc_ref[hp] = lax.dot_general(sk_ref[hp], q[:, hp * PEER_KEYS:(hp + 1) * PEER_KEYS],
                                     (((1,), (1,)), ((), ())), preferred_element_type=F32)

    key_iota = lax.broadcasted_iota(jnp.int32, (PEER_KEYS, t), 0).astype(F32)

    pick_sub = lax.broadcasted_iota(jnp.int32, (PEER_KEYS, PEER_HEADS * PEER_TOPK), 0).astype(F32)
    tokens_per_trip = t // (2 * PEER_HEADS // ROUTE_CHAINS)

    def stage1(it, c):
        hps = [it * ROUTE_CHAINS + n for n in range(ROUTE_CHAINS)]
        outs = [(tv_ref.at[hp], ti_ref.at[hp]) for hp in hps]
        unordered_ties = _top16_columns([sc_ref[hp] for hp in hps], outs)
        for u in range(tokens_per_trip):
            _gate_grid(it * tokens_per_trip + u, pick_sub, p1_ref, p2_ref, pg_ref, stage_ref)

        @pl.when(jnp.max(unordered_ties) > 0.0)
        def _():
            _top16([(sc_ref[hp], val_ref, idx_ref) for hp, (val_ref, idx_ref) in zip(hps, outs)], key_iota)

        return c

    lax.fori_loop(0, 2 * PEER_HEADS // ROUTE_CHAINS, stage1, 0)

    cand_id = _candidate_ids(t)
    rank_iota = lax.broadcasted_iota(jnp.int32, (PEER_TOPK, t), 0).astype(F32)
    low4 = lax.broadcasted_iota(jnp.int32, (SUBLANES, t), 0) < 4

    def stage2(it, c):
        heads = [it * ROUTE_CHAINS + n for n in range(ROUTE_CHAINS)]
        ss, i1s, i2s = [], [], []
        for n, hd in enumerate(heads):
            v1 = tv_ref[2 * hd]
            v2 = tv_ref[2 * hd + 1]
            i1s.append(ti_ref[2 * hd])
            i2s.append(ti_ref[2 * hd + 1])
            v2lo = v2[0:SUBLANES, :]
            v2q = jnp.where(low4, v2lo, pltpu.roll(v2lo, 4, axis=0))
            cand_ref[n, 0:16, :] = v1[0:1, :] + v2
            for r1 in (1, 2, 3):
                cand_ref[n, 8 + 8 * r1:16 + 8 * r1, :] = v1[r1:r1 + 1, :] + v2lo
            cand_ref[n, 40:48, :] = jnp.where(low4, v1[4:5, :], v1[5:6, :]) + v2q
            cand_ref[n, 48:56, :] = jnp.where(low4, v1[6:7, :], v1[7:8, :]) + v2q
            cand_ref[n, 56:64, :] = v1[8:16, :] + v2[0:1, :]
            ss.append(cand_ref[n])
        for r in range(PEER_TOPK):
            for n, hd in enumerate(heads):
                base = pl.multiple_of(hd * PEER_TOPK, PEER_TOPK)
                m = jnp.max(ss[n], axis=0, keepdims=True)
                cid = jnp.min(jnp.where(ss[n] == m, cand_id, float(1 << 20)), axis=0, keepdims=True)
                ss[n] = jnp.where(cand_id == cid, NEG_INF, ss[n])
                r1 = jnp.floor(cid * (1.0 / PEER_TOPK))
                r2 = cid - r1 * PEER_TOPK
                bs_ref[n, r:r + 1, :] = m
                o1_ref[pl.ds(base + r, 1), :] = jnp.max(jnp.where(rank_iota == r1, i1s[n], -1.0), axis=0, keepdims=True)
                o2_ref[pl.ds(base + r, 1), :] = jnp.max(jnp.where(rank_iota == r2, i2s[n], -1.0), axis=0, keepdims=True)
        for n, hd in enumerate(heads):
            base = pl.multiple_of(hd * PEER_TOPK, PEER_TOPK)
            bs = bs_ref[n]
            e = jnp.exp(bs - bs[0:1, :])
            og_ref[pl.ds(base, PEER_TOPK), :] = e / jnp.sum(e, axis=0, keepdims=True)
        rows_per_trip = PEER_KEYS // (PEER_HEADS // ROUTE_CHAINS)
        for u in range(rows_per_trip):
            _gate_rows(it * rows_per_trip + u, t, stage_ref, gd_ref)
        return c

    lax.fori_loop(0, PEER_HEADS // ROUTE_CHAINS, stage2, 0)
    p1_ref[...] = o1_ref[...].T
    p2_ref[...] = o2_ref[...].T
    pg_ref[...] = og_ref[...].T


def _route(x2d, a2d, c2d, wts, *, tile):
    n = x2d.shape[0]
    nt = n // tile
    hk = PEER_HEADS * PEER_TOPK
    row = lambda w: pl.BlockSpec((tile, w), lambda i: (jnp.minimum(i, nt - 1), 0))
    out_shape = (jax.ShapeDtypeStruct((n, D_MODEL), F32),
                 jax.ShapeDtypeStruct((n, D_MODEL), BF16),
                 jax.ShapeDtypeStruct((n, N_EXPERTS), BF16))
    return pl.pallas_call(
        _route_kernel, grid=(nt + 1,),
        in_specs=[row(D_MODEL), row(ATTN_W), row(CONV_CH)] + [_full_spec(w.shape) for w in wts],
        out_specs=(row(D_MODEL), row(D_MODEL),
                   pl.BlockSpec((tile, N_EXPERTS), lambda i: (jnp.maximum(i - 1, 0), 0))),
        out_shape=out_shape,
        scratch_shapes=[pltpu.VMEM((2 * PEER_HEADS, PEER_KEYS, tile), F32),
                        pltpu.VMEM((2 * PEER_HEADS, PEER_TOPK, tile), F32),
                        pltpu.VMEM((2 * PEER_HEADS, PEER_TOPK, tile), F32),
                        pltpu.VMEM((ROUTE_CHAINS, N_CAND, tile), F32),
                        pltpu.VMEM((ROUTE_CHAINS, PEER_TOPK, tile), F32),
                        pltpu.VMEM((hk, tile), F32), pltpu.VMEM((hk, tile), F32), pltpu.VMEM((hk, tile), F32),
                        pltpu.VMEM((tile, hk), F32), pltpu.VMEM((tile, hk), F32), pltpu.VMEM((tile, hk), F32),
                        pltpu.VMEM((tile * GATE_PITCH, PEER_KEYS), F32)],
        compiler_params=_cparams(("arbitrary",)), name="route",
    )(x2d, a2d, c2d, *wts)


def _experts_kernel(pt_ref, h_ref, g_ref, u_ref, v_ref, y_ref, *rest, decode):
    if decode is None:
        o_ref, acc_ref, even_ref, odd_ref = rest
    else:
        q_ref, kn_ref, vn_ref, lfn_ref, su_ref, kt_hbm, vt_hbm, lf_hbm = rest[:8]
        (o_ref, od_ref, acc_ref, even_ref, odd_ref, m_ref, l_ref, dacc_ref, carry_ref,
         kbuf, vbuf, lfbuf, sem) = rest[8:]
        state = (m_ref, l_ref, dacc_ref, carry_ref)
        n_host = pl.num_programs(0) * pl.num_programs(1)
        step = pl.program_id(0) * pl.num_programs(1) + pl.program_id(1)
        active = step < decode.n_steps
        cur = jnp.minimum(step, decode.n_steps - 1)
        nxt = jnp.minimum(step + 1, decode.n_steps - 1)
        seq_step = cur % decode.steps_per_seq
        slot = step % 2
        copies = functools.partial(decode.copies, pt_ref, kt_hbm=kt_hbm, vt_hbm=vt_hbm, lf_hbm=lf_hbm,
                                   kbuf=kbuf, vbuf=vbuf, lfbuf=lfbuf, sem=sem)
    e = pl.program_id(1)

    @pl.when(e == 0)
    def _():
        acc_ref[...] = jnp.zeros_like(acc_ref)
        odd_ref[...] = jnp.zeros_like(odd_ref)

    if decode is not None:
        pl.when(active & (seq_step == 0))(lambda: _decode_init(lfn_ref, *state))

        @pl.when(step == 0)
        def _():
            for c in copies(0, 0):
                c.start(priority=PAGE_COPY_PRIORITY)

    def work(cur_ref, prev_ref):
        if decode is not None:
            for c in copies(nxt, 1 - slot):
                c.start(priority=PAGE_COPY_PRIORITY)
            for c in copies(cur, slot):
                c.wait()
            pages = lambda buf: [buf.at[slot, i] for i in range(decode.pp)]
            _decode_pages(q_ref, su_ref, pages(kbuf), pages(vbuf), pages(lfbuf), *state)
        h = h_ref[...]
        for c in range(u_ref.shape[0] // MXU_TILE):
            cols = slice(c * MXU_TILE, (c + 1) * MXU_TILE)
            s = lax.dot_general(h, u_ref[cols, :], (((1,), (1,)), ((), ())), preferred_element_type=F32)
            act = 0.5 * s * (1.0 + lax.erf(s * math.sqrt(0.5)))
            cur_ref[:, cols] = (g_ref[:, cols].astype(F32) * act).astype(BF16)
        prev = prev_ref[...]
        for c in range(v_ref.shape[1] // MXU_TILE):
            cols = slice(c * MXU_TILE, (c + 1) * MXU_TILE)
            acc_ref[:, cols] += jnp.dot(prev, v_ref[:, cols], preferred_element_type=F32)

    pl.when(e % 2 == 0)(lambda: work(even_ref, odd_ref))
    pl.when(e % 2 == 1)(lambda: work(odd_ref, even_ref))

    if decode is not None:
        pl.when(active & (seq_step == decode.steps_per_seq - 1))(
            lambda: _decode_finish(q_ref, kn_ref, vn_ref, od_ref, m_ref, l_ref, dacc_ref))

        @pl.when(step == n_host - 1)
        def _():
            for c in copies(nxt, 1 - slot):
                c.wait()

    @pl.when(e == pl.num_programs(1) - 1)
    def _():
        o_ref[...] = y_ref[...] + acc_ref[...]


def _experts(h2d, gdense, u_tab, v_tab, y2d, *, tile, eblk, decode=None):
    n = h2d.shape[0]
    nb = N_EXPERTS // eblk
    grid = (n // tile, nb + 1)
    scored = lambda e: jnp.minimum(e, nb - 1)
    folded = lambda e: jnp.maximum(e - 1, 0)
    in_specs = [pl.BlockSpec((tile, D_MODEL), lambda i, e, pt: (i, 0)),
                pl.BlockSpec((tile, eblk), lambda i, e, pt: (i, scored(e))),
                pl.BlockSpec((eblk, D_MODEL), lambda i, e, pt: (scored(e), 0)),
                pl.BlockSpec((eblk, D_MODEL), lambda i, e, pt: (folded(e), 0)),
                pl.BlockSpec((tile, D_MODEL), lambda i, e, pt: (i, 0))]
    out_specs = [pl.BlockSpec((tile, D_MODEL), lambda i, e, pt: (i, 0))]
    out_shape = [jax.ShapeDtypeStruct((n, D_MODEL), F32)]
    scratch = [pltpu.VMEM((tile, D_MODEL), F32), pltpu.VMEM((tile, eblk), BF16), pltpu.VMEM((tile, eblk), BF16)]
    operands = [h2d, gdense, u_tab, v_tab, y2d]
    if decode is None:
        prefetch = jnp.zeros((1, 1), jnp.int32)
    else:
        assert decode.n_steps <= grid[0] * grid[1], "not enough host steps for the paged attention"
        prefetch = decode.page_table
        dec_in, dec_out = decode.specs(lambda i, e: i * grid[1] + e)
        in_specs += dec_in
        out_specs.append(dec_out)
        out_shape.append(decode.out_shape)
        scratch += decode.scratch
        operands += list(decode.operands)
    grid_spec = pltpu.PrefetchScalarGridSpec(
        num_scalar_prefetch=1, grid=grid, in_specs=in_specs, out_specs=out_specs, scratch_shapes=scratch)
    outs = pl.pallas_call(
        functools.partial(_experts_kernel, decode=decode), grid_spec=grid_spec, out_shape=out_shape,
        compiler_params=_cparams(("arbitrary", "arbitrary")), name="experts",
    )(prefetch, *operands)
    return outs[0] if decode is None else tuple(outs)


PROJ_TILE = 512
ATTN_TILE = 512
ATTN_KEY_TILE = 512
ROUTE_TILE = 128
EXPERT_TILE = 1024
EXPERT_BLOCK = 512
DECODE_PAGES_PER_STEP = 8


def _finish(x2d, a2d, c2d, route_wts, u_tab, v_tab, *, expert_tile, decode=None):
    n = x2d.shape[0]
    y, h2, gd = _route(x2d, a2d, c2d, route_wts, tile=min(ROUTE_TILE, n))
    return _experts(h2, gd, u_tab, v_tab, y, tile=expert_tile, eblk=EXPERT_BLOCK, decode=decode)


def kernel(x_prompt, x_sample, cache_k, cache_v, cache_logf, state_conv, page_table, norm1_g, w_in, b_forget,
           q_gain, k_gain, conv_w, conv_b, conv_ln_g, conv_ln_b, w_out, norm2_g, peer_wq, peer_subkeys,
           peer_u, peer_v):
    assert w_in.shape[0] == 1, "single layer"
    batch, seq_len, _ = x_prompt.shape
    db = x_sample.shape[0]
    n_pool, page = cache_k.shape[1], cache_k.shape[2]

    w = w_in[0].astype(BF16)
    w_qkv = w[:, :3 * ATTN_W]
    w_ag = w[:, 3 * ATTN_W:3 * ATTN_W + 2 * CONV_CH]
    w_fg = jnp.pad(w[:, 3 * ATTN_W + 2 * CONV_CH:], ((0, 0), (0, LANES - N_HEADS)))
    b_fg = jnp.pad(b_forget[0].reshape(1, N_HEADS), ((0, 0), (0, LANES - N_HEADS)))
    qg = jnp.tile(q_gain[0], N_HEADS).reshape(1, ATTN_W)
    kg = jnp.tile(k_gain[0], N_HEADS).reshape(1, ATTN_W)
    gi = lax.broadcasted_iota(jnp.int32, (ATTN_W, ATTN_W), 0) // HEAD_DIM
    gj = lax.broadcasted_iota(jnp.int32, (ATTN_W, ATTN_W), 1) // HEAD_DIM
    bd = (gi == gj).astype(BF16)
    tri = (lax.broadcasted_iota(jnp.int32, (LANES, LANES), 1) <=
           lax.broadcasted_iota(jnp.int32, (LANES, LANES), 0)).astype(BF16)
    src = lax.broadcasted_iota(jnp.int32, (ATTN_W, N_HEADS * LANES), 0)
    dst = lax.broadcasted_iota(jnp.int32, (ATTN_W, N_HEADS * LANES), 1)
    place_k = (dst == (src // HEAD_DIM) * LANES + src % HEAD_DIM).astype(BF16)
    src = lax.broadcasted_iota(jnp.int32, (N_BIAS_TERMS * LANES, N_HEADS * LANES), 0)
    dst = lax.broadcasted_iota(jnp.int32, (N_BIAS_TERMS * LANES, N_HEADS * LANES), 1)
    place_c = ((src % LANES < N_HEADS) & (dst == (src % LANES) * LANES + HEAD_DIM + src // LANES)).astype(BF16)
    conv_wts = (conv_w[0], conv_b[0].reshape(1, CONV_CH), conv_ln_g[0].reshape(1, CONV_CH),
                conv_ln_b[0].reshape(1, CONV_CH))
    proj_wts = (norm1_g[0].reshape(1, D_MODEL), w_qkv, w_ag, w_fg, b_fg, qg, kg, bd, tri, place_k, place_c, *conv_wts)
    wo = w_out[0].astype(BF16)
    sk = peer_subkeys[0].astype(BF16).reshape(2 * PEER_HEADS, PEER_KEYS, PEER_KEYS)
    route_wts = (wo[:ATTN_W], wo[ATTN_W:], norm2_g[0].reshape(1, D_MODEL), peer_wq[0].astype(BF16), sk)
    u_tab = peer_u[0].astype(BF16)
    v_tab = peer_v[0].astype(BF16)

    xp2 = x_prompt.reshape(batch * seq_len, D_MODEL)
    k_p, v_p, u_p, lf_p, qb, _, _, kaug, vt, c_p = _proj(
        xp2, proj_wts, tile=PROJ_TILE, seq_len=seq_len, with_cumsum=True)
    xs2 = x_sample.reshape(db, D_MODEL)
    k_s, v_s, u_s, lf_s, qsb, ksb, vsb, _, _, _ = _proj(xs2, proj_wts, tile=db, seq_len=db, with_cumsum=False)

    lf_t = cache_logf[0].transpose(0, 2, 1)
    row = lambda a: a.reshape(db, 1, ATTN_W)
    transposed = lambda c: c[0].transpose(0, 2, 3, 1).reshape(n_pool, ATTN_W, page)
    decode = _DecodeWork(page_table, row(qsb), row(ksb), row(vsb), lf_s.reshape(db, N_HEADS, 1),
                         transposed(cache_k), transposed(cache_v), lf_t, DECODE_PAGES_PER_STEP)

    a_p = _attn(qb, kaug, vt, batch=batch, seq_len=seq_len, tq=ATTN_TILE, tk=ATTN_KEY_TILE)
    y_p, a_s = _finish(xp2, a_p, c_p, route_wts, u_tab, v_tab, expert_tile=EXPERT_TILE, decode=decode)

    window = jnp.concatenate([state_conv[0], u_s[:, None, :]], axis=1)
    c_s = _conv_step(window.transpose(1, 0, 2), *conv_wts)
    y_s = _finish(xs2, a_s.reshape(db, ATTN_W), c_s, route_wts, u_tab, v_tab, expert_tile=db)

    tail = CONV_W - 1
    state = lambda t, seqs, toks: t.reshape(seqs, N_HEADS, HEAD_DIM, toks).transpose(0, 3, 1, 2)[None]
    return (y_p.reshape(batch, seq_len, D_MODEL),
            y_s.reshape(db, 1, D_MODEL),
            state(k_p, batch, seq_len),
            state(v_p, batch, seq_len),
            lf_p.reshape(1, batch, seq_len, N_HEADS),
            u_p.reshape(batch, seq_len, CONV_CH)[:, seq_len - tail:][None],
            state(k_s, 1, db).reshape(1, db, 1, N_HEADS, HEAD_DIM),
            state(v_s, 1, db).reshape(1, db, 1, N_HEADS, HEAD_DIM),
            lf_s.reshape(1, db, 1, N_HEADS),
            window[:, 1:][None])
```
